```python
import math
import jax, jax.numpy as jnp
from jax import lax
import numpy as np

D_MODEL = 1024
BATCH = 8
SEQ = 2048
DEPTH = 1
DEC_BATCH = 128
DEC_SEQ = 4
PAST_LEN = 16384
PAGE_SIZE = 128

A_HEADS = 8
A_DH = 64
A_BLOCK = 256
A_TOPK = 3
A_QBLOCK = 32
A_W = A_HEADS * A_DH
A_SCALE = A_DH ** -0.5
M_HEADS = 8
M_Q_RANK = 256
M_KV_RANK = 128
M_NOPE = 64
M_ROPE = 32
M_DV = 64
M_QK = M_NOPE + M_ROPE
M_W = M_HEADS * M_DV
M_SCALE = M_QK ** -0.5
ROPE_BASE = 10000.0
N_MEM = 256
X_HEADS = 4
X_DH = 128
X_W = X_HEADS * X_DH
D_FF = 2816
N_BRANCH = 3
IN_COLS = 3 * A_W + M_Q_RANK + M_KV_RANK + M_ROPE + X_W + N_BRANCH * D_MODEL
Q_BLOCK = 128
EPS = 1e-6

kernel_name = 'moba_mla_memory_macaron_hybrid_step'


def rmsnorm(x, g):
    xf = x.astype(jnp.float32)
    y = xf * lax.rsqrt(jnp.mean(xf * xf, axis=-1, keepdims=True) + EPS)
    return (y * g.astype(jnp.float32)).astype(x.dtype)


def swiglu(x, w_gate, w_up, w_down):
    return (jax.nn.silu(x @ w_gate) * (x @ w_up)) @ w_down


def alibi_slopes():
    return 2.0 ** (-8.0 * jnp.arange(1, A_HEADS + 1, dtype=jnp.float32) / A_HEADS)


def apply_rope(x, pos):
    half = M_ROPE // 2
    inv = ROPE_BASE ** (-jnp.arange(half, dtype=jnp.float32) / half)
    ang = pos.astype(jnp.float32)[:, None, None] * inv
    c, s = jnp.cos(ang), jnp.sin(ang)
    xf = x.astype(jnp.float32)
    x1, x2 = xf[..., :half], xf[..., half:]
    return jnp.concatenate([x1 * c - x2 * s, x1 * s + x2 * c], axis=-1).astype(x.dtype)


def split_in(z):
    sizes = (A_W, A_W, A_W, M_Q_RANK, M_KV_RANK, M_ROPE, X_W, N_BRANCH * D_MODEL)
    return jnp.split(z, np.cumsum(sizes)[:-1].tolist(), axis=-1)


def project_inputs(u, pos, w_in, g_aq, g_ak, g_mqa, w_uq, g_mq, g_mkv, g_xq):
    lead = u.shape[:-1]
    aq, ak, av, mqa, mkv, mkr, xq, gates = split_in(u @ w_in)
    aq = rmsnorm(aq.reshape(lead + (A_HEADS, A_DH)), g_aq)
    ak = rmsnorm(ak.reshape(lead + (A_HEADS, A_DH)), g_ak)
    av = av.reshape(lead + (A_HEADS, A_DH))
    mq = (rmsnorm(mqa, g_mqa) @ w_uq).reshape(lead + (M_HEADS, M_QK))
    mq = rmsnorm(mq, g_mq)
    mq = jnp.concatenate([mq[..., :M_NOPE], apply_rope(mq[..., M_NOPE:], pos)], axis=-1)
    ckv = rmsnorm(mkv, g_mkv)
    xq = rmsnorm(xq.reshape(lead + (X_HEADS, X_DH)), g_xq)
    return aq, ak, av, mq, ckv, mkr, xq, gates


def mla_keys(ckv, kr, pos, w_ukv, g_mk):
    kv = (ckv @ w_ukv).reshape(ckv.shape[:-1] + (M_HEADS, M_NOPE + M_DV))
    k_nope, v = kv[..., :M_NOPE], kv[..., M_NOPE:]
    k_r = jnp.broadcast_to(kr[..., None, :], k_nope.shape[:-1] + (M_ROPE,))
    k = rmsnorm(jnp.concatenate([k_nope, k_r], axis=-1), g_mk)
    k = jnp.concatenate([k[..., :M_NOPE], apply_rope(k[..., M_NOPE:], pos)], axis=-1)
    return k, v


def memory_kv(mem, g_mem, w_mem_kv, g_xk):
    lead = mem.shape[:-1]
    mk, mv = jnp.split(rmsnorm(mem, g_mem) @ w_mem_kv, 2, axis=-1)
    mk = rmsnorm(mk.reshape(lead + (X_HEADS, X_DH)), g_xk)
    return mk, mv.reshape(lead + (X_HEADS, X_DH))


def memory_attend(xq, mk, mv):
    s = jnp.einsum('bthd,bmhd->bhtm', xq, mk).astype(jnp.float32) * (X_DH ** -0.5)
    p = jax.nn.softmax(s, axis=-1).astype(mv.dtype)
    return jnp.einsum('bhtm,bmhd->bthd', p, mv)


def merge_branches(gates, oa, om, ox, w_pa, w_pm, w_px, w_o):
    ga, gm, gx = jnp.split(jax.nn.sigmoid(gates), N_BRANCH, axis=-1)
    lead = oa.shape[:2]
    ya = oa.reshape(lead + (A_W,)) @ w_pa
    ym = om.reshape(lead + (M_W,)) @ w_pm
    yx = ox.reshape(lead + (X_W,)) @ w_px
    return (ga * ya + gm * ym + gx * yx) @ w_o


def moba_prompt(q, k, v):
    B, S = q.shape[0], q.shape[1]
    nb = -(-S // A_BLOCK)
    pad = nb * A_BLOCK - S
    kb = jnp.pad(k, ((0, 0), (0, pad), (0, 0), (0, 0))).reshape(B, nb, A_BLOCK, A_HEADS, A_DH)
    vb = jnp.pad(v, ((0, 0), (0, pad), (0, 0), (0, 0))).reshape(B, nb, A_BLOCK, A_HEADS, A_DH)
    kmean = jnp.mean(kb.astype(jnp.float32), axis=2)
    kb_h = kb.transpose(0, 3, 1, 2, 4)
    vb_h = vb.transpose(0, 3, 1, 2, 4)
    n_sel = min(A_TOPK, nb)
    slopes = alibi_slopes()
    b_i = jnp.arange(B)[:, None, None, None]
    h_i = jnp.arange(A_HEADS)[None, :, None, None]
    blk = jnp.arange(nb)
    offs = jnp.arange(A_BLOCK)

    def one_block(i):
        t0 = i * A_QBLOCK
        qi = lax.dynamic_slice_in_dim(q, t0, A_QBLOCK, axis=1)
        tpos = t0 + jnp.arange(A_QBLOCK)
        cur = t0 // A_BLOCK
        gate = jnp.einsum('bqhd,bnhd->bhqn', qi.astype(jnp.float32), kmean)
        gate = jnp.where(blk < cur, gate, -jnp.inf)
        top_s, top_i = lax.top_k(gate, n_sel)
        valid = top_s > -jnp.inf
        ks = kb_h[b_i, h_i, top_i]
        vs = vb_h[b_i, h_i, top_i]
        d_sel = (tpos[:, None, None] - (top_i[..., None] * A_BLOCK + offs)).astype(jnp.float32)
        s_sel = jnp.einsum('bqhd,bhqkld->bhqkl', qi, ks).astype(jnp.float32) * A_SCALE - slopes[:, None, None, None] * d_sel
        s_sel = jnp.where(valid[..., None], s_sel, -jnp.inf).reshape(B, A_HEADS, A_QBLOCK, n_sel * A_BLOCK)
        k_own = lax.dynamic_index_in_dim(kb, cur, axis=1, keepdims=False)
        v_own = lax.dynamic_index_in_dim(vb, cur, axis=1, keepdims=False)
        d_own = (tpos[:, None] - (cur * A_BLOCK + offs)[None, :]).astype(jnp.float32)
        s_own = jnp.einsum('bqhd,blhd->bhql', qi, k_own).astype(jnp.float32) * A_SCALE - slopes[:, None, None] * d_own
        s_own = jnp.where(d_own >= 0, s_own, -jnp.inf)
        p = jax.nn.softmax(jnp.concatenate([s_sel, s_own], axis=-1), axis=-1).astype(v.dtype)
        p_sel = p[..., :n_sel * A_BLOCK].reshape(B, A_HEADS, A_QBLOCK, n_sel, A_BLOCK)
        p_own = p[..., n_sel * A_BLOCK:]
        return (jnp.einsum('bhqkl,bhqkld->bqhd', p_sel, vs)
                + jnp.einsum('bhql,blhd->bqhd', p_own, v_own))

    out = lax.map(one_block, jnp.arange(S // A_QBLOCK))
    return out.transpose(1, 0, 2, 3, 4).reshape(B, S, A_HEADS, A_DH)


def mla_prompt(mq, k, v):
    B, S = mq.shape[0], mq.shape[1]
    kpos = jnp.arange(S)

    def one_block(i):
        t0 = i * Q_BLOCK
        qi = lax.dynamic_slice_in_dim(mq, t0, Q_BLOCK, axis=1)
        tpos = t0 + jnp.arange(Q_BLOCK)
        s = jnp.einsum('bqhd,bkhd->bhqk', qi, k).astype(jnp.float32) * M_SCALE
        s = jnp.where(kpos[None, :] <= tpos[:, None], s, -jnp.inf)
        p = jax.nn.softmax(s, axis=-1).astype(v.dtype)
        return jnp.einsum('bhqk,bkhd->bqhd', p, v)

    out = lax.map(one_block, jnp.arange(S // Q_BLOCK))
    return out.transpose(1, 0, 2, 3, 4).reshape(B, S, M_HEADS, M_DV)


def sample_mixers(layer, aq, ak, av, mq, ckv, kr, page_table, cache_moba_k, cache_moba_v,
                  cache_mla_ckv, cache_mla_kr, w_ukv, g_mk):
    P = PAST_LEN
    ppb = A_BLOCK // PAGE_SIZE
    n_full = P // A_BLOCK
    own_start = n_full * A_BLOCK
    n_own_past = P - own_start
    n_sel = min(A_TOPK, n_full)
    ds = aq.shape[1]
    tpos = P + jnp.arange(ds)
    slopes = alibi_slopes()
    offs = jnp.arange(A_BLOCK)
    h_i = jnp.arange(A_HEADS)[:, None, None]
    kpos = jnp.arange(P + ds)

    def one_seq(args):
        q_a, k_a, v_a, q_m, c_m, r_m, pt = args
        k_past = cache_moba_k[layer, pt].reshape(P, A_HEADS, A_DH)
        v_own_past = cache_moba_v[layer, pt[own_start // PAGE_SIZE:]].reshape(n_own_past, A_HEADS, A_DH)
        k_own = jnp.concatenate([k_past[own_start:], k_a], axis=0)
        v_own = jnp.concatenate([v_own_past, v_a], axis=0)
        d_own = (tpos[:, None] - (own_start + jnp.arange(n_own_past + ds))[None, :]).astype(jnp.float32)
        s_own = jnp.einsum('qhd,lhd->hql', q_a, k_own).astype(jnp.float32) * A_SCALE - slopes[:, None, None] * d_own
        s_own = jnp.where(d_own >= 0, s_own, -jnp.inf)
        if n_sel > 0:
            kblk = k_past[:own_start].reshape(n_full, A_BLOCK, A_HEADS, A_DH)
            kmean = jnp.mean(kblk.astype(jnp.float32), axis=1)
            gate = jnp.einsum('qhd,nhd->hqn', q_a.astype(jnp.float32), kmean)
            _, top_i = lax.top_k(gate, n_sel)
            ks = kblk[top_i, :, h_i, :]
            phys = pt[top_i[..., None] * ppb + jnp.arange(ppb)]
            vs = cache_moba_v[layer, phys, :, h_i[..., None], :].reshape(A_HEADS, ds, n_sel, A_BLOCK, A_DH)
            d_sel = (tpos[:, None, None] - (top_i[..., None] * A_BLOCK + offs)).astype(jnp.float32)
            s_sel = jnp.einsum('qhd,hqkld->hqkl', q_a, ks).astype(jnp.float32) * A_SCALE - slopes[:, None, None, None] * d_sel
            p = jax.nn.softmax(jnp.concatenate([s_sel.reshape(A_HEADS, ds, n_sel * A_BLOCK), s_own], axis=-1),
                               axis=-1).astype(v_a.dtype)
            o_a = (jnp.einsum('hqkl,hqkld->qhd', p[..., :n_sel * A_BLOCK].reshape(A_HEADS, ds, n_sel, A_BLOCK), vs)
                   + jnp.einsum('hql,lhd->qhd', p[..., n_sel * A_BLOCK:], v_own))
        else:
            p = jax.nn.softmax(s_own, axis=-1).astype(v_a.dtype)
            o_a = jnp.einsum('hql,lhd->qhd', p, v_own)
        c_all = jnp.concatenate([cache_mla_ckv[layer, pt].reshape(P, M_KV_RANK), c_m], axis=0)
        r_all = jnp.concatenate([cache_mla_kr[layer, pt].reshape(P, M_ROPE), r_m], axis=0)
        k_m, v_m = mla_keys(c_all, r_all, kpos, w_ukv, g_mk)
        s_m = jnp.einsum('qhd,khd->hqk', q_m, k_m).astype(jnp.float32) * M_SCALE
        s_m = jnp.where(kpos[None, :] <= tpos[:, None], s_m, -jnp.inf)
        o_m = jnp.einsum('hqk,khd->qhd', jax.nn.softmax(s_m, axis=-1).astype(v_m.dtype), v_m)
        return o_a, o_m

    return lax.map(one_seq, (aq, ak, av, mq, ckv, kr, page_table))


def layer_pre(x, g_ffn1, w1_gate, w1_up, w1_down, g_mix):
    h = x + 0.5 * swiglu(rmsnorm(x, g_ffn1), w1_gate, w1_up, w1_down)
    return h, rmsnorm(h, g_mix)


def layer_post(h, mix, g_ffn2, w2_gate, w2_up, w2_down):
    h = h + mix
    return h + 0.5 * swiglu(rmsnorm(h, g_ffn2), w2_gate, w2_up, w2_down)


def setup_inputs(seed: int = 0) -> dict:
    key = jax.random.key(seed)
    ks = iter(jax.random.split(key, 48))
    f32 = jnp.float32
    n_pages = PAST_LEN // PAGE_SIZE
    n_used = DEC_BATCH * n_pages
    n_pool = n_used + n_used // 4

    def nrm(shape, scale=1.0):
        return scale * jax.random.normal(next(ks), shape, f32)

    def w(fan_in, fan_out):
        return nrm((DEPTH, fan_in, fan_out), fan_in ** -0.5)

    def g(dim):
        return 1.0 + nrm((DEPTH, dim), 0.05)

    page_table = jax.random.permutation(next(ks), n_pool)[:n_used].reshape(DEC_BATCH, n_pages).astype(jnp.int32)
    return {
        'x_prompt': nrm((BATCH, SEQ, D_MODEL)),
        'mem_prompt': nrm((BATCH, N_MEM, D_MODEL)),
        'x_sample': nrm((DEC_BATCH, DEC_SEQ, D_MODEL)),
        'cache_moba_k': nrm((DEPTH, n_pool, PAGE_SIZE, A_HEADS, A_DH)),
        'cache_moba_v': nrm((DEPTH, n_pool, PAGE_SIZE, A_HEADS, A_DH)),
        'cache_mla_ckv': nrm((DEPTH, n_pool, PAGE_SIZE, M_KV_RANK)),
        'cache_mla_kr': nrm((DEPTH, n_pool, PAGE_SIZE, M_ROPE)),
        'cache_mem_k': nrm((DEPTH, DEC_BATCH, N_MEM, X_HEADS, X_DH)),
        'cache_mem_v': nrm((DEPTH, DEC_BATCH, N_MEM, X_HEADS, X_DH)),
        'page_table': page_table,
        'g_ffn1': g(D_MODEL),
        'w1_gate': w(D_MODEL, D_FF),
        'w1_up': w(D_MODEL, D_FF),
        'w1_down': w(D_FF, D_MODEL),
        'g_mix': g(D_MODEL),
        'w_in': w(D_MODEL, IN_COLS),
        'g_aq': g(A_DH),
        'g_ak': g(A_DH),
        'g_mqa': g(M_Q_RANK),
        'w_uq': w(M_Q_RANK, M_HEADS * M_QK),
        'g_mq': g(M_QK),
        'g_mkv': g(M_KV_RANK),
        'w_ukv': w(M_KV_RANK, M_HEADS * (M_NOPE + M_DV)),
        'g_mk': g(M_QK),
        'g_mem': g(D_MODEL),
        'w_mem_kv': w(D_MODEL, 2 * X_W),
        'g_xq': g(X_DH),
        'g_xk': g(X_DH),
        'w_pa': w(A_W, D_MODEL),
        'w_pm': w(M_W, D_MODEL),
        'w_px': w(X_W, D_MODEL),
        'w_o': w(D_MODEL, D_MODEL),
        'g_ffn2': g(D_MODEL),
        'w2_gate': w(D_MODEL, D_FF),
        'w2_up': w(D_MODEL, D_FF),
        'w2_down': w(D_FF, D_MODEL),
    }


def reference(x_prompt, mem_prompt, x_sample, cache_moba_k, cache_moba_v, cache_mla_ckv, cache_mla_kr,
              cache_mem_k, cache_mem_v, page_table, g_ffn1, w1_gate, w1_up, w1_down, g_mix, w_in,
              g_aq, g_ak, g_mqa, w_uq, g_mq, g_mkv, w_ukv, g_mk, g_mem, w_mem_kv, g_xq, g_xk,
              w_pa, w_pm, w_px, w_o, g_ffn2, w2_gate, w2_up, w2_down):
    pos_p = jnp.arange(x_prompt.shape[1])
    pos_s = PAST_LEN + jnp.arange(x_sample.shape[1])
    yp, ys = x_prompt, x_sample
    ak_p, av_p, ckv_p, kr_p, mk_p, mv_p = [], [], [], [], [], []
    ak_s, av_s, ckv_s, kr_s = [], [], [], []
    for l in range(DEPTH):
        hp, up = layer_pre(yp, g_ffn1[l], w1_gate[l], w1_up[l], w1_down[l], g_mix[l])
        aq, ak, av, mq, ckv, kr, xq, gates = project_inputs(up, pos_p, w_in[l], g_aq[l], g_ak[l], g_mqa[l],
                                                            w_uq[l], g_mq[l], g_mkv[l], g_xq[l])
        mk, mv = memory_kv(mem_prompt, g_mem[l], w_mem_kv[l], g_xk[l])
        oa = moba_prompt(aq, ak, av)
        k_m, v_m = mla_keys(ckv, kr, pos_p, w_ukv[l], g_mk[l])
        om = mla_prompt(mq, k_m, v_m)
        ox = memory_attend(xq, mk, mv)
        mix = merge_branches(gates, oa, om, ox, w_pa[l], w_pm[l], w_px[l], w_o[l])
        yp = layer_post(hp, mix, g_ffn2[l], w2_gate[l], w2_up[l], w2_down[l])
        ak_p.append(ak); av_p.append(av); ckv_p.append(ckv); kr_p.append(kr); mk_p.append(mk); mv_p.append(mv)
        hs, us = layer_pre(ys, g_ffn1[l], w1_gate[l], w1_up[l], w1_down[l], g_mix[l])
        aq2, ak2, av2, mq2, ckv2, kr2, xq2, gates2 = project_inputs(us, pos_s, w_in[l], g_aq[l], g_ak[l], g_mqa[l],
                                                                    w_uq[l], g_mq[l], g_mkv[l], g_xq[l])
        oa2, om2 = sample_mixers(l, aq2, ak2, av2, mq2, ckv2, kr2, page_table, cache_moba_k, cache_moba_v,
                                 cache_mla_ckv, cache_mla_kr, w_ukv[l], g_mk[l])
        ox2 = memory_attend(xq2, cache_mem_k[l], cache_mem_v[l])
        mix2 = merge_branches(gates2, oa2, om2, ox2, w_pa[l], w_pm[l], w_px[l], w_o[l])
        ys = layer_post(hs, mix2, g_ffn2[l], w2_gate[l], w2_up[l], w2_down[l])
        ak_s.append(ak2); av_s.append(av2); ckv_s.append(ckv2); kr_s.append(kr2)
    new_moba_k_prompt = jnp.stack(ak_p)
    new_moba_v_prompt = jnp.stack(av_p)
    new_mla_ckv_prompt = jnp.stack(ckv_p)
    new_mla_kr_prompt = jnp.stack(kr_p)
    new_mem_k_prompt = jnp.stack(mk_p)
    new_mem_v_prompt = jnp.stack(mv_p)
    new_moba_k_sample = jnp.stack(ak_s)
    new_moba_v_sample = jnp.stack(av_s)
    new_mla_ckv_sample = jnp.stack(ckv_s)
    new_mla_kr_sample = jnp.stack(kr_s)
    return (yp, ys, new_moba_k_prompt, new_moba_v_prompt, new_mla_ckv_prompt, new_mla_kr_prompt,
            new_mem_k_prompt, new_mem_v_prompt, new_moba_k_sample, new_moba_v_sample,
            new_mla_ckv_sample, new_mla_kr_sample)
```

```python
import functools

import numpy as np
import jax
import jax.numpy as jnp
from jax import lax
from jax.experimental import pallas as pl
from jax.experimental.pallas import tpu as pltpu

F32 = jnp.float32
BF16 = jnp.bfloat16

EPS = 1e-6
ROPE_BASE = 10000.0
PAGE_SIZE = 128
A_HEADS, A_DH, A_BLOCK, A_TOPK = 8, 64, 256, 3
A_W = A_HEADS * A_DH
A_SCALE = A_DH ** -0.5
M_HEADS, M_Q_RANK, M_KV_RANK, M_NOPE, M_ROPE, M_DV = 8, 256, 128, 64, 32, 64
M_QK = M_NOPE + M_ROPE
M_HALF = M_ROPE // 2
M_W = M_HEADS * M_DV
M_QW = M_HEADS * M_QK
M_SCALE = M_QK ** -0.5
X_HEADS, X_DH = 4, 128
X_W = X_HEADS * X_DH
X_SCALE = X_DH ** -0.5
N_BRANCH = 3
LANES = 128
VMEM_LIMIT = 56 * 1024 * 1024
NEG_INF = float("-inf")
M_INIT = -1e30
PAGES_PER_STEP = 8
LANE_CHUNK = 2048

_NT = (((1,), (1,)), ((), ()))


def _dot(a, b):
    return jnp.dot(a, b, preferred_element_type=F32)


def _dot_nt(a, b):
    return lax.dot_general(a, b, _NT, preferred_element_type=F32)


def _cparams(*sem):
    return pltpu.CompilerParams(dimension_semantics=sem, vmem_limit_bytes=VMEM_LIMIT)


def _rms(x, g):
    return x * lax.rsqrt(jnp.mean(x * x, axis=-1, keepdims=True) + EPS) * g


def _split_hi_lo(x):
    hi = x.astype(BF16)
    lo = (x - hi.astype(F32)).astype(BF16)
    return hi, lo


def _top3_mask(g, col, n):
    sel = jnp.zeros_like(g)
    for _ in range(A_TOPK):
        m = jnp.max(g, axis=1, keepdims=True)
        first = jnp.min(jnp.where(g == m, col, float(n)), axis=1, keepdims=True)
        pick = col == first
        sel = jnp.where(jnp.logical_and(pick, m > NEG_INF), 1.0, sel)
        g = jnp.where(pick, NEG_INF, g)
    return sel


def _ffn_kernel(x_ref, g_ref, wg_ref, wu_ref, wd_ref, g2_ref, *rest, with_u):
    if with_u:
        y_ref, u_ref, xn_sc, acc_sc = rest
    else:
        y_ref, xn_sc, acc_sc = rest
    k = pl.program_id(1)

    @pl.when(k == 0)
    def _():
        xn_sc[...] = _rms(x_ref[...], g_ref[...]).astype(BF16)
        acc_sc[...] = jnp.zeros_like(acc_sc)

    xn = xn_sc[...]
    gate = _dot(xn, wg_ref[...])
    up = _dot(xn, wu_ref[...])
    hid = gate * jax.nn.sigmoid(gate) * up
    acc_sc[...] += _dot(hid.astype(BF16), wd_ref[...])

    @pl.when(k == pl.num_programs(1) - 1)
    def _():
        y = x_ref[...] + 0.5 * acc_sc[...]
        y_ref[...] = y
        if with_u:
            u_ref[...] = _rms(y, g2_ref[...]).astype(BF16)


def _ffn(x, g, wg, wu, wd, g2, with_u):
    t, d = x.shape
    dff = wg.shape[1]
    tm = min(t, 512)
    tf = dff // 2 if (dff // 2) % LANES == 0 else dff
    grid = (t // tm, dff // tf)
    tok = pl.BlockSpec((tm, d), lambda i, k: (i, 0))
    vec = pl.BlockSpec((1, d), lambda i, k: (0, 0))
    out_shape = [jax.ShapeDtypeStruct((t, d), F32)]
    out_specs = [tok]
    if with_u:
        out_shape.append(jax.ShapeDtypeStruct((t, d), BF16))
        out_specs.append(tok)
    res = pl.pallas_call(
        functools.partial(_ffn_kernel, with_u=with_u),
        grid=grid,
        in_specs=[tok, vec,
                  pl.BlockSpec((d, tf), lambda i, k: (0, k)),
                  pl.BlockSpec((d, tf), lambda i, k: (0, k)),
                  pl.BlockSpec((tf, d), lambda i, k: (k, 0)),
                  vec],
        out_specs=out_specs,
        out_shape=out_shape,
        scratch_shapes=[pltpu.VMEM((tm, d), BF16), pltpu.VMEM((tm, d), F32)],
        compiler_params=_cparams("parallel", "arbitrary"),
        name="ffn",
    )(x, g, wg, wu, wd, g2)
    return res if with_u else res[0]


def _project_kernel(u_ref, cos_ref, sin_ref, wp_ref, e64_ref, e96_ref, wuq_ref, wukv_ref, tkr_ref,
                    gaq_ref, gak_ref, gmqa_ref, gmq_ref, gmkv_ref, gmk_ref, gxq_ref,
                    aq_ref, akf_ref, akb_ref, avf_ref, avb_ref, mq_ref, ckv_ref, kr_ref, xq_ref,
                    km_ref, vm_ref, kmean_ref, *, tokens_on_lanes):
    tm = u_ref.shape[0]

    def store_cache_rows(ref, x):
        if tokens_on_lanes:
            ref[0] = x.T[:ref.shape[1]]
        else:
            ref[...] = x[:, :ref.shape[1]]

    z = _dot(u_ref[...], wp_ref[...])
    cs = cos_ref[...]
    sn = sin_ref[...]

    def group_norm(x, e_ref, n):
        ss = _dot((x * x).astype(BF16), e_ref[...])
        return x * lax.rsqrt(ss * (1.0 / n) + EPS)

    def rope_store(xn, out_ref):
        x1 = xn[:, A_W:A_W + LANES]
        x2 = xn[:, A_W + LANES:]
        out_ref[:, :A_W] = xn[:, :A_W].astype(BF16)
        out_ref[:, A_W:A_W + LANES] = (x1 * cs - x2 * sn).astype(BF16)
        out_ref[:, A_W + LANES:] = (x1 * sn + x2 * cs).astype(BF16)

    aq = group_norm(z[:, 0:A_W], e64_ref, A_DH) * gaq_ref[...]
    aq_ref[...] = aq.astype(BF16)
    ak = group_norm(z[:, A_W:2 * A_W], e64_ref, A_DH) * gak_ref[...]
    store_cache_rows(akf_ref, ak)
    akb_ref[...] = ak.astype(BF16)
    for blk in range(tm // A_BLOCK):
        kmean_ref[blk] = jnp.mean(ak[blk * A_BLOCK:(blk + 1) * A_BLOCK], axis=0, keepdims=True)
    av = z[:, 2 * A_W:3 * A_W]
    store_cache_rows(avf_ref, av)
    avb_ref[...] = av.astype(BF16)

    o = 3 * A_W
    mqa = _rms(z[:, o:o + M_Q_RANK], gmqa_ref[...])
    mq = _dot(mqa.astype(BF16), wuq_ref[...])
    rope_store(group_norm(mq, e96_ref, M_QK) * gmq_ref[...], mq_ref)

    o += M_Q_RANK
    ckv = _rms(z[:, o:o + M_KV_RANK], gmkv_ref[...])
    ckv_ref[...] = ckv
    o += M_KV_RANK
    xq = z[:, o:o + X_W]
    o += X_W
    kr = z[:, o:o + M_ROPE]
    store_cache_rows(kr_ref, z[:, o:o + LANES])
    kv = _dot(ckv.astype(BF16), wukv_ref[...])
    vm_ref[...] = kv[:, M_W:].astype(BF16)
    kr_hi, kr_lo = _split_hi_lo(kr)
    kr_t = _dot(kr_hi, tkr_ref[...]) + _dot(kr_lo, tkr_ref[...])
    kpre = jnp.concatenate([kv[:, :M_W], kr_t], axis=1)
    rope_store(group_norm(kpre, e96_ref, M_QK) * gmk_ref[...], km_ref)

    for h in range(X_HEADS):
        xh = xq[:, h * X_DH:(h + 1) * X_DH]
        xq_ref[:, h * X_DH:(h + 1) * X_DH] = _rms(xh, gxq_ref[...]).astype(BF16)


def _project(u, cos_t, sin_t, w, tm, batch=None):
    t = u.shape[0]
    n_rope_blocks = cos_t.shape[0] // tm
    grid = (t // tm,)
    nb = tm // A_BLOCK
    cache_rows = ("ak", "av", "kr")

    def tok(width):
        return pl.BlockSpec((tm, width), lambda i: (i, 0))

    def out_shape_spec(name, width, dt):
        if batch is None or name not in cache_rows:
            return jax.ShapeDtypeStruct((t, width), dt), tok(width)
        per_seq = t // batch // tm
        return (jax.ShapeDtypeStruct((batch, width, t // batch), dt),
                pl.BlockSpec((1, width, tm), lambda i: (i // per_seq, 0, i % per_seq)))

    def full(a):
        return pl.BlockSpec(a.shape, lambda i: (0,) * a.ndim)

    rope = pl.BlockSpec((tm, LANES), lambda i: (i % n_rope_blocks, 0))
    consts = [w["wp"], w["e64"], w["e96"], w["wuq"], w["wukv"], w["tkr"],
              w["g_aq"], w["g_ak"], w["g_mqa"], w["g_mq"], w["g_mkv"], w["g_mk"], w["g_xq"]]
    outs = [("aq", A_W, BF16), ("ak", A_W, F32), ("ak_b", A_W, BF16), ("av", A_W, F32), ("av_b", A_W, BF16),
            ("mq", M_QW, BF16), ("ckv", M_KV_RANK, F32), ("kr", M_ROPE, F32), ("xq", X_W, BF16),
            ("km", M_QW, BF16), ("vm", M_W, BF16)]
    out_shape, out_specs = (list(x) for x in zip(*(out_shape_spec(*o) for o in outs)))
    out_shape.append(jax.ShapeDtypeStruct((t // A_BLOCK, 1, A_W), F32))
    out_specs.append(pl.BlockSpec((nb, 1, A_W), lambda i: (i, 0, 0)))
    res = pl.pallas_call(
        functools.partial(_project_kernel, tokens_on_lanes=batch is not None),
        grid=grid,
        in_specs=[tok(u.shape[1]), rope, rope] + [full(a) for a in consts],
        out_specs=out_specs,
        out_shape=out_shape,
        compiler_params=_cparams("parallel"),
        name="project",
    )(u, cos_t, sin_t, *consts)
    named = {name: r for (name, _, _), r in zip(outs, res[:-1])}
    named["kmean"] = res[-1]
    return named


def _flash_update(carry, s, v):
    m, l, acc = carry
    m_new = jnp.maximum(m, jnp.max(s, axis=1, keepdims=True))
    alpha = jnp.exp(m - m_new)
    p = jnp.exp(s - m_new)
    l = alpha * l + jnp.sum(p, axis=1, keepdims=True)
    acc = alpha * acc + _dot(p.astype(BF16), v)
    return m_new, l, acc


def _flash_init(tq):
    return (jnp.full((tq, 1), M_INIT, F32), jnp.zeros((tq, 1), F32), jnp.zeros((tq, LANES), F32))


def _moba_kernel(slope_ref, q_ref, k_ref, v_ref, km_ref, o_ref):
    tq = q_ref.shape[1]
    pair = pl.program_id(1)
    i = pl.program_id(2)
    qf = q_ref[0].astype(F32)
    lane = lax.broadcasted_iota(jnp.int32, (tq, LANES), 1)
    rel = (lax.broadcasted_iota(jnp.int32, (tq, tq), 0)
           - lax.broadcasted_iota(jnp.int32, (tq, tq), 1)).astype(F32)
    nblk = km_ref.shape[1]
    colb = lax.broadcasted_iota(jnp.int32, (tq, nblk), 1).astype(F32)
    i_f = i.astype(F32)
    outs = []
    for e in range(2):
        in_head = (lane < A_DH) if e == 0 else (lane >= A_DH)
        qe = jnp.where(in_head, qf, 0.0).astype(BF16)
        slope = slope_ref[2 * pair + e]
        gate = _dot_nt(qe, km_ref[0])
        sel = _top3_mask(jnp.where(colb < i_f, gate, NEG_INF), colb, nblk)
        srel = slope * rel

        def past_block(j, carry, qe=qe, sel=sel, srel=srel, slope=slope):
            off = pl.multiple_of(j * tq, tq)
            kj = k_ref[0, pl.ds(off, tq), :]
            vj = v_ref[0, pl.ds(off, tq), :]
            dist = ((i - j) * tq).astype(F32)
            s = _dot_nt(qe, kj) * A_SCALE - (srel + slope * dist)
            selj = jnp.sum(jnp.where(colb == j.astype(F32), sel, 0.0), axis=1, keepdims=True)
            return _flash_update(carry, jnp.where(selj > 0.5, s, NEG_INF), vj)

        carry = lax.fori_loop(0, i, past_block, _flash_init(tq))
        off = pl.multiple_of(i * tq, tq)
        s = _dot_nt(qe, k_ref[0, pl.ds(off, tq), :]) * A_SCALE - srel
        m, l, acc = _flash_update(carry, jnp.where(rel >= 0.0, s, NEG_INF), v_ref[0, pl.ds(off, tq), :])
        outs.append(acc / l)
    o_ref[0] = jnp.where(lane < A_DH, outs[0], outs[1]).astype(BF16)


def _moba_prompt(aq, ak, av, kmean, slopes):
    b, s, _ = aq.shape
    tq = A_BLOCK
    qspec = pl.BlockSpec((1, tq, LANES), lambda bi, p, i: (bi, i, p))
    kspec = pl.BlockSpec((1, s, LANES), lambda bi, p, i: (bi, 0, p))
    return pl.pallas_call(
        _moba_kernel,
        grid=(b, A_W // LANES, s // tq),
        in_specs=[pl.BlockSpec(memory_space=pltpu.SMEM), qspec, kspec, kspec,
                  pl.BlockSpec((1, s // A_BLOCK, LANES), lambda bi, p, i: (bi, 0, p))],
        out_specs=qspec,
        out_shape=jax.ShapeDtypeStruct((b, s, A_W), BF16),
        compiler_params=_cparams("parallel", "parallel", "arbitrary"),
        name="moba_prompt",
    )(slopes, aq, ak, av, kmean)


def _mla_kernel(qn_ref, q1_ref, q2_ref, kn_ref, k1_ref, k2_ref, v_ref, o_ref):
    tq = qn_ref.shape[1]
    pair = pl.program_id(1)
    i = pl.program_id(2)
    qn = qn_ref[0].astype(F32)
    q1 = q1_ref[0].astype(F32)
    q2 = q2_ref[0].astype(F32)
    lane = lax.broadcasted_iota(jnp.int32, (tq, LANES), 1)
    rel = (lax.broadcasted_iota(jnp.int32, (tq, tq), 0)
           - lax.broadcasted_iota(jnp.int32, (tq, tq), 1))
    outs = []
    for e in range(2):
        head = 2 * pair + e
        in_head = (lane < M_NOPE) if e == 0 else (lane >= M_NOPE)
        in_rope = jnp.logical_and(lane >= head * M_HALF, lane < (head + 1) * M_HALF)
        qe = jnp.concatenate([jnp.where(in_head, qn, 0.0), jnp.where(in_rope, q1, 0.0),
                              jnp.where(in_rope, q2, 0.0)], axis=1).astype(BF16)

        def kv_block(off):
            kj = jnp.concatenate([kn_ref[0, pl.ds(off, tq), :], k1_ref[0, pl.ds(off, tq), :],
                                  k2_ref[0, pl.ds(off, tq), :]], axis=1)
            return kj, v_ref[0, pl.ds(off, tq), :]

        def past_block(j, carry, qe=qe):
            kj, vj = kv_block(pl.multiple_of(j * tq, tq))
            return _flash_update(carry, _dot_nt(qe, kj) * M_SCALE, vj)

        carry = lax.fori_loop(0, i, past_block, _flash_init(tq))
        kj, vj = kv_block(pl.multiple_of(i * tq, tq))
        s = jnp.where(rel >= 0, _dot_nt(qe, kj) * M_SCALE, NEG_INF)
        m, l, acc = _flash_update(carry, s, vj)
        outs.append(acc / l)
    o_ref[0] = jnp.where(lane < M_DV, outs[0], outs[1]).astype(BF16)


def _mla_prompt(mq, km, vm):
    b, s, _ = mq.shape
    tq = A_BLOCK
    x1, x2 = A_W // LANES, A_W // LANES + 1

    def qspec(blk):
        return pl.BlockSpec((1, tq, LANES), lambda bi, p, i: (bi, i, p if blk is None else blk))

    def kspec(blk):
        return pl.BlockSpec((1, s, LANES), lambda bi, p, i: (bi, 0, p if blk is None else blk))

    return pl.pallas_call(
        _mla_kernel,
        grid=(b, M_W // LANES, s // tq),
        in_specs=[qspec(None), qspec(x1), qspec(x2), kspec(None), kspec(x1), kspec(x2), kspec(None)],
        out_specs=qspec(None),
        out_shape=jax.ShapeDtypeStruct((b, s, M_W), BF16),
        compiler_params=_cparams("parallel", "parallel", "arbitrary"),
        name="mla_prompt",
    )(mq, mq, mq, km, km, km, vm)


def _memkv_kernel(mem_ref, gmem_ref, w_ref, gxk_ref, mk_ref, mv_ref):
    kv = _dot(_rms(mem_ref[...], gmem_ref[...]).astype(BF16), w_ref[...])
    for h in range(X_HEADS):
        mk_ref[:, h * X_DH:(h + 1) * X_DH] = _rms(kv[:, h * X_DH:(h + 1) * X_DH], gxk_ref[...])
    mv_ref[...] = kv[:, X_W:]


def _memory_kv(mem, g_mem, w, g_xk):
    t, d = mem.shape
    tm = min(t, 512)
    out = jax.ShapeDtypeStruct((t, X_W), F32)
    ospec = pl.BlockSpec((tm, X_W), lambda i: (i, 0))
    return pl.pallas_call(
        _memkv_kernel,
        grid=(t // tm,),
        in_specs=[pl.BlockSpec((tm, d), lambda i: (i, 0)), pl.BlockSpec((1, d), lambda i: (0, 0)),
                  pl.BlockSpec(w.shape, lambda i: (0, 0)), pl.BlockSpec((1, X_DH), lambda i: (0, 0))],
        out_specs=[ospec, ospec],
        out_shape=[out, out],
        compiler_params=_cparams("parallel"),
        name="memory_kv",
    )(mem, g_mem, w, g_xk)


def _cross_kernel(q_ref, k_ref, v_ref, o_ref):
    for h in range(X_HEADS):
        sl = slice(h * X_DH, (h + 1) * X_DH)
        s = _dot_nt(q_ref[0, :, sl], k_ref[0, :, sl].astype(BF16)) * X_SCALE
        p = jnp.exp(s - jnp.max(s, axis=1, keepdims=True))
        p = p / jnp.sum(p, axis=1, keepdims=True)
        o_ref[0, :, sl] = _dot(p.astype(BF16), v_ref[0, :, sl].astype(BF16)).astype(BF16)


def _cross_prompt(xq, mk, mv):
    b, s, _ = xq.shape
    n_mem = mk.shape[1]
    tq = min(s, 512)
    qspec = pl.BlockSpec((1, tq, X_W), lambda bi, i: (bi, i, 0))
    kspec = pl.BlockSpec((1, n_mem, X_W), lambda bi, i: (bi, 0, 0))
    return pl.pallas_call(
        _cross_kernel,
        grid=(b, s // tq),
        in_specs=[qspec, kspec, kspec],
        out_specs=qspec,
        out_shape=jax.ShapeDtypeStruct((b, s, X_W), BF16),
        compiler_params=_cparams("parallel", "parallel"),
        name="cross_prompt",
    )(xq, mk, mv)


def _merge_kernel(u_ref, oa_ref, om_ref, ox_ref, h_ref, wg_ref, wpa_ref, wpm_ref, wpx_ref, wo_ref, y_ref):
    d = h_ref.shape[1]
    u = u_ref[...]
    comb = None
    for br, (o_ref, wp_ref) in enumerate(((oa_ref, wpa_ref), (om_ref, wpm_ref), (ox_ref, wpx_ref))):
        gate = jax.nn.sigmoid(_dot(u, wg_ref[:, br * d:(br + 1) * d]))
        term = gate * _dot(o_ref[...].astype(BF16), wp_ref[...])
        comb = term if comb is None else comb + term
    y_ref[...] = h_ref[...] + _dot(comb.astype(BF16), wo_ref[...])


def _merge(u, oa, om, ox, h, w):
    t, d = h.shape
    tm = min(t, 512)

    def tok(a):
        return pl.BlockSpec((tm, a.shape[1]), lambda i: (i, 0))

    def full(a):
        return pl.BlockSpec(a.shape, lambda i: (0, 0))

    consts = [w["w_gates"], w["w_pa"], w["w_pm"], w["w_px"], w["w_o"]]
    return pl.pallas_call(
        _merge_kernel,
        grid=(t // tm,),
        in_specs=[tok(u), tok(oa), tok(om), tok(ox), tok(h)] + [full(a) for a in consts],
        out_specs=tok(h),
        out_shape=jax.ShapeDtypeStruct((t, d), F32),
        compiler_params=_cparams("parallel"),
        name="merge",
    )(u, oa, om, ox, h, *consts)


def _fold_heads(res, head_mask, row_q, n_q):
    masked = res * head_mask
    return [jnp.sum(jnp.where(row_q == float(q), masked, 0.0), axis=0, keepdims=True) for q in range(n_q)]


def _paged_scores_kernel(pt_ref, *refs, n_pages, past_len):
    del pt_ref
    npp = PAGES_PER_STEP
    kp_refs = refs[:npp]
    ck_refs = refs[npp:2 * npp]
    kr_refs = refs[2 * npp:3 * npp]
    (qa_ref, qn_ref, qm_ref, ar_ref, ka_ref, km_ref, vm_ref, cos_ref, sin_ref, wkn_ref, wv_ref, gn_ref,
     hm5_ref, hm7_ref, rowc_ref, exp_ref,
     pa_ref, pown_ref, om_ref,
     sa_sc, sm_sc, ckv_sc, g_sc, aq_sc) = refs[3 * npp:]
    c = pl.program_id(1)
    nrow = sa_sc.shape[0]
    n_q = nrow // A_HEADS
    n_blocks = g_sc.shape[1]
    colg = lax.broadcasted_iota(jnp.int32, (nrow, n_blocks), 1)

    @pl.when(c == 0)
    def _():
        g_sc[...] = jnp.zeros_like(g_sc)
        qn = qn_ref[0].astype(F32) * gn_ref[...] * hm5_ref[...]
        aq_sc[...] = _dot(qn.astype(BF16), wkn_ref[...]).astype(BF16)

    qa = (qa_ref[0].astype(F32) * hm5_ref[...]).astype(BF16)
    aq = aq_sc[...]
    ar = ar_ref[0].astype(BF16)
    for pi in range(npp):
        gp = c * npp + pi
        off = pl.multiple_of(gp * PAGE_SIZE, PAGE_SIZE)
        s_a = _dot(qa, kp_refs[pi][0].astype(BF16))
        sa_sc[:, pl.ds(off, PAGE_SIZE)] = s_a
        blk = gp // (A_BLOCK // PAGE_SIZE)
        g_sc[...] += jnp.where(colg == blk, jnp.sum(s_a, axis=1, keepdims=True), 0.0)
        ckb = ck_refs[pi][0].astype(BF16)
        kn_t = _dot_nt(wkn_ref[...], ckb)
        ssq = jnp.sum((kn_t * kn_t).reshape(M_HEADS, M_NOPE, PAGE_SIZE), axis=1)
        kr_t = kr_refs[pi][0]
        ssr = jnp.sum(kr_t * kr_t, axis=0, keepdims=True)
        rinv = lax.rsqrt((ssq + ssr) * (1.0 / M_QK) + EPS)
        lsl = slice(pi * PAGE_SIZE, (pi + 1) * PAGE_SIZE)
        feat = jnp.concatenate([kr_t * cos_ref[:, lsl], kr_t * sin_ref[:, lsl]], axis=0).astype(BF16)
        s_m = _dot_nt(aq, ckb) + _dot(ar, feat)
        sm_sc[:, pl.ds(off, PAGE_SIZE)] = s_m * jnp.concatenate([rinv] * n_q, axis=0)
        ckv_sc[pl.ds(off, PAGE_SIZE), :] = ckb

    @pl.when(c == pl.num_programs(1) - 1)
    def _():
        slope = rowc_ref[:, 0:1]
        row_q = rowc_ref[:, 1:2]
        tpos = row_q + float(past_len)
        lc = min(LANE_CHUNK, past_len)
        n_ch = past_len // lc
        lane_new = lax.broadcasted_iota(jnp.int32, (nrow, LANES), 1).astype(F32)
        new_ok = jnp.logical_and(lane_new <= row_q, lane_new < float(n_q))

        sel = _top3_mask(g_sc[...], colg.astype(F32), n_blocks).astype(BF16)
        s_own = _dot_nt(qa, ka_ref[0]) * A_SCALE - slope * (row_q - lane_new)
        s_own = jnp.where(new_ok, s_own, NEG_INF)
        mx = jnp.max(s_own, axis=1, keepdims=True)
        for ch in range(n_ch):
            sl = slice(ch * lc, (ch + 1) * lc)
            kpos = (lax.broadcasted_iota(jnp.int32, (nrow, lc), 1) + ch * lc).astype(F32)
            s = sa_sc[:, sl] * A_SCALE - slope * (tpos - kpos)
            s = jnp.where(_dot(sel, exp_ref[:, sl]) > 0.5, s, NEG_INF)
            sa_sc[:, sl] = s
            mx = jnp.maximum(mx, jnp.max(s, axis=1, keepdims=True))
        p_own = jnp.exp(s_own - mx)
        den = jnp.sum(p_own, axis=1, keepdims=True)
        for ch in range(n_ch):
            sl = slice(ch * lc, (ch + 1) * lc)
            p = jnp.exp(sa_sc[:, sl] - mx)
            sa_sc[:, sl] = p
            den = den + jnp.sum(p, axis=1, keepdims=True)
        for ch in range(n_ch):
            sl = slice(ch * lc, (ch + 1) * lc)
            pa_ref[0, :, sl] = (sa_sc[:, sl] / den).astype(BF16)
        pown_ref[0] = p_own / den

        qm = (qm_ref[0].astype(F32) * hm7_ref[...]).astype(BF16)
        s_new = jnp.where(new_ok, _dot_nt(qm, km_ref[0]) * M_SCALE, NEG_INF)
        mx = jnp.max(s_new, axis=1, keepdims=True)
        for ch in range(n_ch):
            sl = slice(ch * lc, (ch + 1) * lc)
            s = sm_sc[:, sl] * M_SCALE
            sm_sc[:, sl] = s
            mx = jnp.maximum(mx, jnp.max(s, axis=1, keepdims=True))
        p_new = jnp.exp(s_new - mx)
        den = jnp.sum(p_new, axis=1, keepdims=True)
        for ch in range(n_ch):
            sl = slice(ch * lc, (ch + 1) * lc)
            p = jnp.exp(sm_sc[:, sl] - mx)
            sm_sc[:, sl] = p
            den = den + jnp.sum(p, axis=1, keepdims=True)
        pc = jnp.zeros((nrow, M_KV_RANK), F32)
        for ch in range(n_ch):
            sl = slice(ch * lc, (ch + 1) * lc)
            pc = pc + _dot((sm_sc[:, sl] / den).astype(BF16), ckv_sc[sl, :])
        pc_hi, pc_lo = _split_hi_lo(pc)
        res = (_dot(pc_hi, wv_ref[...]) + _dot(pc_lo, wv_ref[...])
               + _dot((p_new / den).astype(BF16), vm_ref[0]))
        rows = _fold_heads(res, hm5_ref[...], row_q, n_q)
        for q in range(n_q):
            om_ref[0, q:q + 1, :] = rows[q]


def _paged_values_kernel(pt_ref, *refs):
    del pt_ref
    npp = PAGES_PER_STEP
    vp_refs = refs[:npp]
    pa_ref, pown_ref, va_ref, hm5_ref, rowc_ref, oa_ref, acc_sc = refs[npp:]
    c = pl.program_id(1)
    n_q = oa_ref.shape[1]

    @pl.when(c == 0)
    def _():
        acc_sc[...] = _dot(pown_ref[0].astype(BF16), va_ref[0])

    acc = acc_sc[...]
    for pi in range(npp):
        lsl = slice(pi * PAGE_SIZE, (pi + 1) * PAGE_SIZE)
        acc = acc + _dot_nt(pa_ref[0, :, lsl], vp_refs[pi][0].astype(BF16))
    acc_sc[...] = acc

    @pl.when(c == pl.num_programs(1) - 1)
    def _():
        rows = _fold_heads(acc, hm5_ref[...], rowc_ref[:, 1:2], n_q)
        for q in range(n_q):
            oa_ref[0, q:q + 1, :] = rows[q]


def _sample_mixers(page_table, cache_k, cache_v, cache_ckv, cache_kr, s, consts):
    nseq, n_pages = page_table.shape
    past_len = n_pages * PAGE_SIZE
    npp = PAGES_PER_STEP
    n_steps = n_pages // npp
    nrow = s["qa_rep"].shape[1]
    n_q = nrow // A_HEADS
    n_blocks = past_len // A_BLOCK
    chunk = npp * PAGE_SIZE

    def page_spec(rows, cols, pi):
        return pl.BlockSpec((1, rows, cols), lambda b, c, pt: (pt[b, c * npp + pi], 0, 0))

    def seq_spec(a):
        return pl.BlockSpec((1,) + a.shape[1:], lambda b, c, pt: (b, 0, 0))

    def full(a):
        return pl.BlockSpec(a.shape, lambda b, c, pt: (0,) * a.ndim)

    table = pl.BlockSpec((M_ROPE, chunk), lambda b, c, pt: (0, c))
    seq_in = [s["qa_rep"], s["qn_rep"], s["qm_rep"], s["ar"], s["ka_new"], s["km_new"], s["vm_new"]]
    const_in = [consts["wkn_t"], consts["wv"], consts["g_nope"], consts["hm5"], consts["hm7"], consts["rowc"],
                consts["expand"]]
    pa, pown, om = pl.pallas_call(
        functools.partial(_paged_scores_kernel, n_pages=n_pages, past_len=past_len),
        grid_spec=pltpu.PrefetchScalarGridSpec(
            num_scalar_prefetch=1,
            grid=(nseq, n_steps),
            in_specs=([page_spec(A_W, PAGE_SIZE, pi) for pi in range(npp)]
                      + [page_spec(PAGE_SIZE, M_KV_RANK, pi) for pi in range(npp)]
                      + [page_spec(M_ROPE, PAGE_SIZE, pi) for pi in range(npp)]
                      + [seq_spec(a) for a in seq_in] + [table, table] + [full(a) for a in const_in]),
            out_specs=[pl.BlockSpec((1, nrow, past_len), lambda b, c, pt: (b, 0, 0)),
                       pl.BlockSpec((1, nrow, LANES), lambda b, c, pt: (b, 0, 0)),
                       pl.BlockSpec((1, n_q, M_W), lambda b, c, pt: (b, 0, 0))],
            scratch_shapes=[pltpu.VMEM((nrow, past_len), F32), pltpu.VMEM((nrow, past_len), F32),
                            pltpu.VMEM((past_len, M_KV_RANK), BF16), pltpu.VMEM((nrow, n_blocks), F32),
                            pltpu.VMEM((nrow, M_KV_RANK), BF16)]),
        out_shape=[jax.ShapeDtypeStruct((nseq, nrow, past_len), BF16),
                   jax.ShapeDtypeStruct((nseq, nrow, LANES), F32),
                   jax.ShapeDtypeStruct((nseq, n_q, M_W), F32)],
        compiler_params=_cparams("parallel", "arbitrary"),
        name="paged_scores",
    )(page_table, *([cache_k] * npp), *([cache_ckv] * npp), *([cache_kr] * npp), *seq_in,
      consts["cos_t"], consts["sin_t"], *const_in)

    oa = pl.pallas_call(
        _paged_values_kernel,
        grid_spec=pltpu.PrefetchScalarGridSpec(
            num_scalar_prefetch=1,
            grid=(nseq, n_steps),
            in_specs=([page_spec(A_W, PAGE_SIZE, pi) for pi in range(npp)]
                      + [pl.BlockSpec((1, nrow, chunk), lambda b, c, pt: (b, 0, c)),
                         seq_spec(pown), seq_spec(s["va_new"]), full(consts["hm5"]), full(consts["rowc"])]),
            out_specs=pl.BlockSpec((1, n_q, A_W), lambda b, c, pt: (b, 0, 0)),
            scratch_shapes=[pltpu.VMEM((nrow, A_W), F32)]),
        out_shape=jax.ShapeDtypeStruct((nseq, n_q, A_W), F32),
        compiler_params=_cparams("parallel", "arbitrary"),
        name="paged_values",
    )(page_table, *([cache_v] * npp), pa, pown, s["va_new"], consts["hm5"], consts["rowc"])
    return oa, om


def _cross_sample_kernel(q_ref, k_ref, v_ref, o_ref):
    nrow = q_ref.shape[1]
    ncol = k_ref.shape[1]
    s = _dot_nt(q_ref[0], k_ref[0].astype(BF16)) * X_SCALE
    row_h = lax.broadcasted_iota(jnp.int32, (nrow, ncol), 0) % X_HEADS
    col_h = lax.broadcasted_iota(jnp.int32, (nrow, ncol), 1) % X_HEADS
    s = jnp.where(row_h == col_h, s, NEG_INF)
    p = jnp.exp(s - jnp.max(s, axis=1, keepdims=True))
    p = p / jnp.sum(p, axis=1, keepdims=True)
    o_ref[0] = _dot(p.astype(BF16), v_ref[0].astype(BF16))


def _cross_sample(xq_rows, mem_k, mem_v):
    nseq, nrow, _ = xq_rows.shape
    ncol = mem_k.shape[1]
    kspec = pl.BlockSpec((1, ncol, X_DH), lambda b: (b, 0, 0))
    qspec = pl.BlockSpec((1, nrow, X_DH), lambda b: (b, 0, 0))
    return pl.pallas_call(
        _cross_sample_kernel,
        grid=(nseq,),
        in_specs=[qspec, kspec, kspec],
        out_specs=qspec,
        out_shape=jax.ShapeDtypeStruct((nseq, nrow, X_DH), F32),
        compiler_params=_cparams("parallel"),
        name="cross_sample",
    )(xq_rows, mem_k, mem_v)


def _mla_perm():
    nope = [h * M_QK + d for h in range(M_HEADS) for d in range(M_NOPE)]
    lo = [h * M_QK + M_NOPE + j for h in range(M_HEADS) for j in range(M_HALF)]
    hi = [h * M_QK + M_NOPE + M_HALF + j for h in range(M_HEADS) for j in range(M_HALF)]
    return np.array(nope + lo + hi, np.int32)


def _static_consts():
    perm = _mla_perm()
    head7 = perm // M_QK
    dim7 = perm % M_QK
    e64 = (np.arange(A_W)[:, None] // A_DH == np.arange(A_W)[None, :] // A_DH)
    e96 = head7[:, None] == head7[None, :]
    tkr = np.zeros((M_ROPE, 2 * LANES), np.float32)
    for h in range(M_HEADS):
        for j in range(M_HALF):
            tkr[j, h * M_HALF + j] = 1.0
            tkr[M_HALF + j, LANES + h * M_HALF + j] = 1.0
    return perm, head7, dim7, e64.astype(np.float32), e96.astype(np.float32), tkr


def _rope_tables(pos):
    inv = ROPE_BASE ** (-jnp.arange(M_HALF, dtype=F32) / M_HALF)
    ang = pos.astype(F32)[:, None] * inv
    return jnp.cos(ang), jnp.sin(ang)


def _row_consts(n_q, heads, width, head_of_lane):
    r = np.arange(n_q * heads)
    hm = (head_of_lane[None, :] == (r % heads)[:, None]).astype(np.float32)
    rowc = np.zeros((n_q * heads, LANES), np.float32)
    rowc[:, 0] = 2.0 ** (-8.0 * ((r % heads) + 1) / heads)
    rowc[:, 1] = r // heads
    assert hm.shape[1] == width
    return hm, rowc


def kernel(x_prompt, mem_prompt, x_sample, cache_moba_k, cache_moba_v, cache_mla_ckv, cache_mla_kr, cache_mem_k, cache_mem_v, page_table, g_ffn1, w1_gate, w1_up, w1_down, g_mix, w_in, g_aq, g_ak, g_mqa, w_uq, g_mq, g_mkv, w_ukv, g_mk, g_mem, w_mem_kv, g_xq, g_xk, w_pa, w_pm, w_px, w_o, g_ffn2, w2_gate, w2_up, w2_down):
    depth = w_in.shape[0]
    assert depth == 1
    b, s, d = x_prompt.shape
    nseq, n_q, _ = x_sample.shape
    n_mem = mem_prompt.shape[1]
    n_pages = page_table.shape[1]
    past_len = n_pages * PAGE_SIZE
    assert past_len % A_BLOCK == 0 and past_len // A_BLOCK >= A_TOPK and n_pages % PAGES_PER_STEP == 0
    assert s % 512 == 0 and (nseq * n_q) % A_BLOCK == 0
    l = 0

    perm, head7, dim7, e64, e96, tkr = _static_consts()

    def row(g):
        return g.reshape(1, -1).astype(F32)

    w_in_l = w_in[l]
    o = 3 * A_W + M_Q_RANK + M_KV_RANK
    kr_cols = w_in_l[:, o:o + M_ROPE]
    xq_cols = w_in_l[:, o + M_ROPE:o + M_ROPE + X_W]
    gate_cols = w_in_l[:, o + M_ROPE + X_W:]
    n_proj = o + X_W + M_ROPE
    pad = (-n_proj) % LANES
    wp = jnp.concatenate([w_in_l[:, :o], xq_cols, kr_cols, jnp.zeros((d, pad), F32)], axis=1).astype(BF16)
    w_ukv_l = w_ukv[l].reshape(M_KV_RANK, M_HEADS, M_NOPE + M_DV)
    wkn = w_ukv_l[:, :, :M_NOPE].reshape(M_KV_RANK, M_W)
    wv = w_ukv_l[:, :, M_NOPE:].reshape(M_KV_RANK, M_W)
    pw = {
        "wp": wp,
        "e64": jnp.asarray(e64, BF16),
        "e96": jnp.asarray(e96, BF16),
        "wuq": w_uq[l][:, perm].astype(BF16),
        "wukv": jnp.concatenate([wkn, wv], axis=1).astype(BF16),
        "tkr": jnp.asarray(tkr, BF16),
        "g_aq": row(jnp.tile(g_aq[l], A_HEADS)),
        "g_ak": row(jnp.tile(g_ak[l], A_HEADS)),
        "g_mqa": row(g_mqa[l]),
        "g_mq": row(g_mq[l][dim7]),
        "g_mkv": row(g_mkv[l]),
        "g_mk": row(g_mk[l][dim7]),
        "g_xq": row(g_xq[l]),
    }
    mw = {"w_gates": gate_cols.astype(BF16), "w_pa": w_pa[l].astype(BF16), "w_pm": w_pm[l].astype(BF16),
          "w_px": w_px[l].astype(BF16), "w_o": w_o[l].astype(BF16)}
    ffn1 = (row(g_ffn1[l]), w1_gate[l].astype(BF16), w1_up[l].astype(BF16), w1_down[l].astype(BF16), row(g_mix[l]))
    ffn2 = (row(g_ffn2[l]), w2_gate[l].astype(BF16), w2_up[l].astype(BF16), w2_down[l].astype(BF16), row(g_mix[l]))
    slopes = 2.0 ** (-8.0 * jnp.arange(1, A_HEADS + 1, dtype=F32) / A_HEADS)

    def lane_tables(pos):
        c, sn = _rope_tables(pos)
        return jnp.tile(c, (1, M_HEADS)), jnp.tile(sn, (1, M_HEADS))

    t = b * s
    hp, up = _ffn(x_prompt.reshape(t, d), *ffn1, with_u=True)
    cos_p, sin_p = lane_tables(jnp.arange(s))
    pr = _project(up, cos_p, sin_p, pw, tm=512, batch=b)
    mk, mv = _memory_kv(mem_prompt.reshape(b * n_mem, d), row(g_mem[l]), w_mem_kv[l].astype(BF16), row(g_xk[l]))

    def bsd(a):
        return a.reshape(b, s, a.shape[-1])

    oa = _moba_prompt(bsd(pr["aq"]), bsd(pr["ak_b"]), bsd(pr["av_b"]),
                      pr["kmean"].reshape(b, s // A_BLOCK, A_W).astype(BF16), slopes)
    om = _mla_prompt(bsd(pr["mq"]), bsd(pr["km"]), bsd(pr["vm"]))
    ox = _cross_prompt(bsd(pr["xq"]), mk.reshape(b, n_mem, X_W), mv.reshape(b, n_mem, X_W))
    h2 = _merge(up, oa.reshape(t, A_W), om.reshape(t, M_W), ox.reshape(t, X_W), hp, mw)
    yp = _ffn(h2, *ffn2, with_u=False).reshape(b, s, d)

    ts = nseq * n_q
    hs, us = _ffn(x_sample.reshape(ts, d), *ffn1, with_u=True)
    pos_s = past_len + (jnp.arange(ts) % n_q)
    cos_s, sin_s = lane_tables(pos_s)
    ps = _project(us, cos_s, sin_s, pw, tm=min(ts, 512))

    nrow = n_q * A_HEADS
    hm5, rowc = _row_consts(n_q, A_HEADS, A_W, np.arange(A_W) // A_DH)
    hm7, _ = _row_consts(n_q, M_HEADS, M_QW, np.where(np.arange(M_QW) < M_W, np.arange(M_QW) // M_NOPE,
                                                        (np.arange(M_QW) % LANES) // M_HALF))

    def rep_rows(a, heads):
        return jnp.repeat(a.reshape(nseq, n_q, a.shape[-1]), heads, axis=1)

    def pad_new(a):
        a = a.reshape(nseq, n_q, a.shape[-1])
        return jnp.pad(a, ((0, 0), (0, LANES - n_q), (0, 0)))

    mq_s = ps["mq"].astype(F32)
    q_lo = mq_s[:, A_W:A_W + LANES].reshape(nseq, n_q, M_HEADS, M_HALF)
    q_hi = mq_s[:, A_W + LANES:].reshape(nseq, n_q, M_HEADS, M_HALF)
    g_r = g_mk[l][M_NOPE:].astype(F32)
    ar_cos = jnp.concatenate([q_lo, q_hi], axis=-1) * g_r
    ar_sin = jnp.concatenate([q_hi, -q_lo], axis=-1) * g_r
    ar = jnp.concatenate([ar_cos, ar_sin], axis=-1).reshape(nseq, nrow, 2 * M_ROPE)

    c_all, s_all = _rope_tables(jnp.arange(past_len))
    n_blocks = past_len // A_BLOCK
    expand = (np.arange(past_len)[None, :] // A_BLOCK == np.arange(n_blocks)[:, None]).astype(np.float32)
    sc = {
        "wkn_t": wkn.T.astype(BF16),
        "wv": wv.astype(BF16),
        "g_nope": row(jnp.tile(g_mk[l][:M_NOPE], M_HEADS)),
        "hm5": jnp.asarray(hm5), "hm7": jnp.asarray(hm7), "rowc": jnp.asarray(rowc),
        "expand": jnp.asarray(expand, BF16),
        "cos_t": jnp.tile(c_all.T, (2, 1)), "sin_t": jnp.tile(s_all.T, (2, 1)),
    }
    sq = {
        "qa_rep": rep_rows(ps["aq"], A_HEADS),
        "qn_rep": rep_rows(ps["mq"][:, :M_W], M_HEADS),
        "qm_rep": rep_rows(ps["mq"], M_HEADS),
        "ar": ar,
        "ka_new": pad_new(ps["ak_b"]), "va_new": pad_new(ps["av_b"]),
        "km_new": pad_new(ps["km"]), "vm_new": pad_new(ps["vm"]),
    }
    n_pool = cache_moba_k.shape[1]

    def keys_on_lanes(c):
        c = c[l]
        return jnp.moveaxis(c, 1, -1).reshape(n_pool, -1, PAGE_SIZE)

    oa2, om2 = _sample_mixers(page_table, keys_on_lanes(cache_moba_k), keys_on_lanes(cache_moba_v),
                              cache_mla_ckv[l], keys_on_lanes(cache_mla_kr), sq, sc)
    ox2 = _cross_sample(ps["xq"].reshape(nseq, n_q * X_HEADS, X_DH),
                        cache_mem_k[l].reshape(nseq, n_mem * X_HEADS, X_DH),
                        cache_mem_v[l].reshape(nseq, n_mem * X_HEADS, X_DH))
    hs2 = _merge(us, oa2.reshape(ts, A_W), om2.reshape(ts, M_W), ox2.reshape(ts, X_W), hs, mw)
    ys = _ffn(hs2, *ffn2, with_u=False).reshape(nseq, n_q, d)

    def stack(a, shape):
        return a.reshape((1,) + shape)

    def tokens_back(a, shape):
        return jnp.moveaxis(a.reshape((b,) + shape + (s,)), -1, 1)[None]

    return (yp, ys,
            tokens_back(pr["ak"], (A_HEADS, A_DH)), tokens_back(pr["av"], (A_HEADS, A_DH)),
            stack(pr["ckv"], (b, s, M_KV_RANK)), tokens_back(pr["kr"], (M_ROPE,)),
            stack(mk, (b, n_mem, X_HEADS, X_DH)), stack(mv, (b, n_mem, X_HEADS, X_DH)),
            stack(ps["ak"], (nseq, n_q, A_HEADS, A_DH)), stack(ps["av"], (nseq, n_q, A_HEADS, A_DH)),
            stack(ps["ckv"], (nseq, n_q, M_KV_RANK)), stack(ps["kr"], (nseq, n_q, M_ROPE)))
```

```python
import functools

import numpy as np
import jax
import jax.numpy as jnp
from jax import lax
from jax.experimental import pallas as pl
from jax.experimental.pallas import tpu as pltpu

F32 = jnp.float32
BF16 = jnp.bfloat16

EPS = 1e-6
ROPE_BASE = 10000.0
PAGE_SIZE = 128
A_HEADS, A_DH, A_BLOCK, A_TOPK = 8, 64, 256, 3
A_W = A_HEADS * A_DH
A_SCALE = A_DH ** -0.5
M_HEADS, M_Q_RANK, M_KV_RANK, M_NOPE, M_ROPE, M_DV = 8, 256, 128, 64, 32, 64
M_QK = M_NOPE + M_ROPE
M_HALF = M_ROPE // 2
M_W = M_HEADS * M_DV
M_QW = M_HEADS * M_QK
M_SCALE = M_QK ** -0.5
X_HEADS, X_DH = 4, 128
X_W = X_HEADS * X_DH
X_SCALE = X_DH ** -0.5
N_BRANCH = 3
LANES = 128
VMEM_LIMIT = 56 * 1024 * 1024
NEG_INF = float("-inf")
SCORE_PAGES_PER_STEP = 32
VALUE_PAGES_PER_STEP = 32
GROUP_PAGES = 8
LANE_CHUNK = 2048

_NT = (((1,), (1,)), ((), ()))


def _dot(a, b):
    return jnp.dot(a, b, preferred_element_type=F32)


def _dot_nt(a, b):
    return lax.dot_general(a, b, _NT, preferred_element_type=F32)


def _cparams(*sem):
    return pltpu.CompilerParams(dimension_semantics=sem, vmem_limit_bytes=VMEM_LIMIT)


def _rms(x, g):
    return x * lax.rsqrt(jnp.mean(x * x, axis=-1, keepdims=True) + EPS) * g


def _split_hi_lo(x):
    hi = x.astype(BF16)
    lo = (x - hi.astype(F32)).astype(BF16)
    return hi, lo


def _top3_mask(g, idx, n, axis=1):
    sel = jnp.zeros_like(g)
    for _ in range(A_TOPK):
        m = jnp.max(g, axis=axis, keepdims=True)
        first = jnp.min(jnp.where(g == m, idx, float(n)), axis=axis, keepdims=True)
        pick = idx == first
        sel = jnp.where(jnp.logical_and(pick, m > NEG_INF), 1.0, sel)
        g = jnp.where(pick, NEG_INF, g)
    return sel


def _ffn_kernel(x_ref, g_ref, wg_ref, wu_ref, wd_ref, g2_ref, *rest, with_u):
    if with_u:
        y_ref, u_ref, xn_sc, acc_sc = rest
    else:
        y_ref, xn_sc, acc_sc = rest
    k = pl.program_id(1)

    @pl.when(k == 0)
    def _():
        xn_sc[...] = _rms(x_ref[...], g_ref[...]).astype(BF16)
        acc_sc[...] = jnp.zeros_like(acc_sc)

    xn = xn_sc[...]
    gate = _dot(xn, wg_ref[...])
    up = _dot(xn, wu_ref[...])
    hid = gate * jax.nn.sigmoid(gate) * up
    acc_sc[...] += _dot(hid.astype(BF16), wd_ref[...])

    @pl.when(k == pl.num_programs(1) - 1)
    def _():
        y = x_ref[...] + 0.5 * acc_sc[...]
        y_ref[...] = y
        if with_u:
            u_ref[...] = _rms(y, g2_ref[...]).astype(BF16)


def _ffn(x, g, wg, wu, wd, g2, with_u):
    t, d = x.shape
    dff = wg.shape[1]
    tm = min(t, 512)
    tf = dff // 2 if (dff // 2) % LANES == 0 else dff
    grid = (t // tm, dff // tf)
    tok = pl.BlockSpec((tm, d), lambda i, k: (i, 0))
    vec = pl.BlockSpec((1, d), lambda i, k: (0, 0))
    out_shape = [jax.ShapeDtypeStruct((t, d), F32)]
    out_specs = [tok]
    if with_u:
        out_shape.append(jax.ShapeDtypeStruct((t, d), BF16))
        out_specs.append(tok)
    res = pl.pallas_call(
        functools.partial(_ffn_kernel, with_u=with_u),
        grid=grid,
        in_specs=[tok, vec,
                  pl.BlockSpec((d, tf), lambda i, k: (0, k)),
                  pl.BlockSpec((d, tf), lambda i, k: (0, k)),
                  pl.BlockSpec((tf, d), lambda i, k: (k, 0)),
                  vec],
        out_specs=out_specs,
        out_shape=out_shape,
        scratch_shapes=[pltpu.VMEM((tm, d), BF16), pltpu.VMEM((tm, d), F32)],
        compiler_params=_cparams("parallel", "arbitrary"),
        name="ffn",
    )(x, g, wg, wu, wd, g2)
    return res if with_u else res[0]


def _project_kernel(u_ref, cos_ref, sin_ref, wp_ref, e64_ref, e96_ref, wuq_ref, wukv_ref, tkr_ref,
                    gaq_ref, gak_ref, gmqa_ref, gmq_ref, gmkv_ref, gmk_ref, gxq_ref,
                    aq_ref, akf_ref, akb_ref, avf_ref, avb_ref, mq_ref, ckv_ref, kr_ref, xq_ref,
                    km_ref, vm_ref, kmean_ref, *, tokens_on_lanes):
    tm = u_ref.shape[0]

    def store_rows(x, *out_refs):
        if tokens_on_lanes:
            x = x.T
        for ref in out_refs:
            if tokens_on_lanes:
                ref[0] = x[:ref.shape[1]].astype(ref.dtype)
            else:
                ref[...] = x[:, :ref.shape[1]].astype(ref.dtype)

    z = _dot(u_ref[...], wp_ref[...])
    cs = cos_ref[...]
    sn = sin_ref[...]

    def group_norm(x, e_ref, n):
        ss = _dot((x * x).astype(BF16), e_ref[...])
        return x * lax.rsqrt(ss * (1.0 / n) + EPS)

    def rope_store(xn, out_ref):
        x1 = xn[:, A_W:A_W + LANES]
        x2 = xn[:, A_W + LANES:]
        out_ref[:, :A_W] = xn[:, :A_W].astype(BF16)
        out_ref[:, A_W:A_W + LANES] = (x1 * cs - x2 * sn).astype(BF16)
        out_ref[:, A_W + LANES:] = (x1 * sn + x2 * cs).astype(BF16)

    aq = group_norm(z[:, 0:A_W], e64_ref, A_DH) * gaq_ref[...]
    aq_ref[...] = aq.astype(BF16)
    ak = group_norm(z[:, A_W:2 * A_W], e64_ref, A_DH) * gak_ref[...]
    store_rows(ak, akf_ref)
    akb_ref[...] = ak.astype(BF16)
    for blk in range(tm // A_BLOCK):
        kmean_ref[blk] = jnp.mean(ak[blk * A_BLOCK:(blk + 1) * A_BLOCK], axis=0, keepdims=True)
    av = z[:, 2 * A_W:3 * A_W]
    store_rows(av, avf_ref, avb_ref)

    o = 3 * A_W
    mqa = _rms(z[:, o:o + M_Q_RANK], gmqa_ref[...])
    mq = _dot(mqa.astype(BF16), wuq_ref[...])
    rope_store(group_norm(mq, e96_ref, M_QK) * gmq_ref[...], mq_ref)

    o += M_Q_RANK
    ckv = _rms(z[:, o:o + M_KV_RANK], gmkv_ref[...])
    ckv_ref[...] = ckv
    o += M_KV_RANK
    xq = z[:, o:o + X_W]
    o += X_W
    kr = z[:, o:o + M_ROPE]
    store_rows(z[:, o:o + LANES], kr_ref)
    kv = _dot(ckv.astype(BF16), wukv_ref[...])
    store_rows(kv[:, M_W:], vm_ref)
    kr_hi, kr_lo = _split_hi_lo(kr)
    kr_t = _dot(kr_hi, tkr_ref[...]) + _dot(kr_lo, tkr_ref[...])
    kpre = jnp.concatenate([kv[:, :M_W], kr_t], axis=1)
    rope_store(group_norm(kpre, e96_ref, M_QK) * gmk_ref[...], km_ref)

    for h in range(X_HEADS):
        xh = xq[:, h * X_DH:(h + 1) * X_DH]
        xq_ref[:, h * X_DH:(h + 1) * X_DH] = _rms(xh, gxq_ref[...]).astype(BF16)


def _project(u, cos_t, sin_t, w, tm, batch=None):
    t = u.shape[0]
    n_rope_blocks = cos_t.shape[0] // tm
    grid = (t // tm,)
    nb = tm // A_BLOCK
    cache_rows = ("ak", "av", "av_b", "kr", "vm")

    def tok(width):
        return pl.BlockSpec((tm, width), lambda i: (i, 0))

    def out_shape_spec(name, width, dt):
        if batch is None or name not in cache_rows:
            return jax.ShapeDtypeStruct((t, width), dt), tok(width)
        per_seq = t // batch // tm
        return (jax.ShapeDtypeStruct((batch, width, t // batch), dt),
                pl.BlockSpec((1, width, tm), lambda i: (i // per_seq, 0, i % per_seq)))

    def full(a):
        return pl.BlockSpec(a.shape, lambda i: (0,) * a.ndim)

    rope = pl.BlockSpec((tm, LANES), lambda i: (i % n_rope_blocks, 0))
    consts = [w["wp"], w["e64"], w["e96"], w["wuq"], w["wukv"], w["tkr"],
              w["g_aq"], w["g_ak"], w["g_mqa"], w["g_mq"], w["g_mkv"], w["g_mk"], w["g_xq"]]
    outs = [("aq", A_W, BF16), ("ak", A_W, F32), ("ak_b", A_W, BF16), ("av", A_W, F32), ("av_b", A_W, BF16),
            ("mq", M_QW, BF16), ("ckv", M_KV_RANK, F32), ("kr", M_ROPE, F32), ("xq", X_W, BF16),
            ("km", M_QW, BF16), ("vm", M_W, BF16)]
    out_shape, out_specs = (list(x) for x in zip(*(out_shape_spec(*o) for o in outs)))
    out_shape.append(jax.ShapeDtypeStruct((t // A_BLOCK, 1, A_W), F32))
    out_specs.append(pl.BlockSpec((nb, 1, A_W), lambda i: (i, 0, 0)))
    res = pl.pallas_call(
        functools.partial(_project_kernel, tokens_on_lanes=batch is not None),
        grid=grid,
        in_specs=[tok(u.shape[1]), rope, rope] + [full(a) for a in consts],
        out_specs=out_specs,
        out_shape=out_shape,
        compiler_params=_cparams("parallel"),
        name="project",
    )(u, cos_t, sin_t, *consts)
    named = {name: r for (name, _, _), r in zip(outs, res[:-1])}
    named["kmean"] = res[-1]
    return named


def _attend_blocks(n_blocks, tq, score_block, v_t_ref, rows, s_sc, p_sc):
    m = None
    for b in range(n_blocks):
        s_t = score_block(b)
        s_sc[b * tq:(b + 1) * tq, :] = s_t
        bm = jnp.max(s_t, axis=0, keepdims=True)
        m = bm if m is None else jnp.maximum(m, bm)
    l = None
    for b in range(n_blocks):
        p = jnp.exp(s_sc[b * tq:(b + 1) * tq, :] - m)
        p_sc[b * tq:(b + 1) * tq, :] = p.astype(BF16)
        bl = jnp.sum(p, axis=0, keepdims=True)
        l = bl if l is None else l + bl
    nk = n_blocks * tq
    return _dot(v_t_ref[0, rows, 0:nk], p_sc[0:nk, :]) / l


def _per_tile_index(i, n_tiles, body):
    for ii in range(n_tiles):
        @pl.when(i == ii)
        def _(ii=ii):
            body(ii)


def _head_rows(e, dv):
    return slice(e * dv, (e + 1) * dv)


def _moba_kernel(slope_ref, q_ref, k_ref, vt_ref, km_ref, o_ref, s_sc, p_sc):
    tq = q_ref.shape[1]
    pair = pl.program_id(1)
    i = pl.program_id(2)
    nblk = km_ref.shape[1]
    qf = q_ref[0].astype(F32) * A_SCALE
    lane = lax.broadcasted_iota(jnp.int32, (tq, LANES), 1)
    rel_t = (lax.broadcasted_iota(jnp.int32, (tq, tq), 1)
             - lax.broadcasted_iota(jnp.int32, (tq, tq), 0)).astype(F32)
    rowb = lax.broadcasted_iota(jnp.int32, (nblk, tq), 0).astype(F32)
    i_f = i.astype(F32)
    qs, sels, slopes = [], [], []
    for e in range(2):
        in_head = (lane < A_DH) if e == 0 else (lane >= A_DH)
        qe = jnp.where(in_head, qf, 0.0).astype(BF16)
        gate_t = _dot_nt(km_ref[0], qe)
        sels.append(_top3_mask(jnp.where(rowb < i_f, gate_t, NEG_INF), rowb, nblk, axis=0))
        qs.append(qe)
        slopes.append(slope_ref[2 * pair + e])

    def body(ii):
        nk = (ii + 1) * tq
        outs = []
        for e in range(2):
            s_all = _dot_nt(k_ref[0, 0:nk, :], qs[e])
            bias = slopes[e] * rel_t

            def score_block(b, e=e, s_all=s_all, bias=bias):
                s_t = s_all[b * tq:(b + 1) * tq] - bias - slopes[e] * float((ii - b) * tq)
                keep = (rel_t >= 0.0) if b == ii else (sels[e][b:b + 1] > 0.5)
                return jnp.where(keep, s_t, NEG_INF)

            outs.append(_attend_blocks(ii + 1, tq, score_block, vt_ref, _head_rows(e, A_DH),
                                       s_sc.at[e], p_sc.at[e]))
        o_ref[0] = jnp.concatenate(outs, axis=0).T.astype(BF16)

    _per_tile_index(i, nblk, body)


def _attn_scratch(s, tq):
    return [pltpu.VMEM((2, s, tq), F32), pltpu.VMEM((2, s, tq), BF16)]


def _moba_prompt(aq, ak, av_t, kmean, slopes):
    b, s, _ = aq.shape
    tq = A_BLOCK
    nblk = s // A_BLOCK
    qspec = pl.BlockSpec((1, tq, LANES), lambda bi, p, i: (bi, i, p))
    return pl.pallas_call(
        _moba_kernel,
        grid=(b, A_W // LANES, s // tq),
        in_specs=[pl.BlockSpec(memory_space=pltpu.SMEM), qspec,
                  pl.BlockSpec((1, s, LANES), lambda bi, p, i: (bi, 0, p)),
                  pl.BlockSpec((1, LANES, s), lambda bi, p, i: (bi, p, 0)),
                  pl.BlockSpec((1, nblk, LANES), lambda bi, p, i: (bi, 0, p))],
        out_specs=qspec,
        out_shape=jax.ShapeDtypeStruct((b, s, A_W), BF16),
        scratch_shapes=_attn_scratch(s, tq),
        compiler_params=_cparams("parallel", "parallel", "arbitrary"),
        name="moba_prompt",
    )(slopes, aq, ak, av_t, kmean)


def _mla_kernel(qn_ref, qr_ref, kn_ref, kr_ref, vt_ref, o_ref, s_sc, p_sc):
    tq = qn_ref.shape[1]
    pair = pl.program_id(1)
    i = pl.program_id(2)
    n_tiles = kn_ref.shape[1] // tq
    qn = qn_ref[0].astype(F32)
    qr = qr_ref[0].astype(F32)
    lane = lax.broadcasted_iota(jnp.int32, (tq, LANES), 1)
    lane_r = lax.broadcasted_iota(jnp.int32, (tq, 2 * LANES), 1) % LANES
    rel_t = (lax.broadcasted_iota(jnp.int32, (tq, tq), 1)
             - lax.broadcasted_iota(jnp.int32, (tq, tq), 0))
    qs = []
    for e in range(2):
        head = 2 * pair + e
        in_head = (lane < M_NOPE) if e == 0 else (lane >= M_NOPE)
        in_rope = jnp.logical_and(lane_r >= head * M_HALF, lane_r < (head + 1) * M_HALF)
        qs.append([jnp.where(in_head, qn, 0.0).astype(BF16), jnp.where(in_rope, qr, 0.0).astype(BF16)])

    def body(ii):
        nk = (ii + 1) * tq
        outs = []
        for e in range(2):
            s_all = (_dot_nt(kn_ref[0, 0:nk, :], qs[e][0]) + _dot_nt(kr_ref[0, 0:nk, :], qs[e][1])) * M_SCALE

            def score_block(b, s_all=s_all):
                s_t = s_all[b * tq:(b + 1) * tq]
                return jnp.where(rel_t >= 0, s_t, NEG_INF) if b == ii else s_t

            outs.append(_attend_blocks(ii + 1, tq, score_block, vt_ref, _head_rows(e, M_DV),
                                       s_sc.at[e], p_sc.at[e]))
        o_ref[0] = jnp.concatenate(outs, axis=0).T.astype(BF16)

    _per_tile_index(i, n_tiles, body)


def _mla_prompt(mq, km, vm_t):
    b, s, _ = mq.shape
    tq = A_BLOCK
    rope_blk = M_W // (2 * LANES)
    qspec = pl.BlockSpec((1, tq, LANES), lambda bi, p, i: (bi, i, p))
    return pl.pallas_call(
        _mla_kernel,
        grid=(b, M_W // LANES, s // tq),
        in_specs=[qspec, pl.BlockSpec((1, tq, 2 * LANES), lambda bi, p, i: (bi, i, rope_blk)),
                  pl.BlockSpec((1, s, LANES), lambda bi, p, i: (bi, 0, p)),
                  pl.BlockSpec((1, s, 2 * LANES), lambda bi, p, i: (bi, 0, rope_blk)),
                  pl.BlockSpec((1, LANES, s), lambda bi, p, i: (bi, p, 0))],
        out_specs=qspec,
        out_shape=jax.ShapeDtypeStruct((b, s, M_W), BF16),
        scratch_shapes=_attn_scratch(s, tq),
        compiler_params=_cparams("parallel", "parallel", "arbitrary"),
        name="mla_prompt",
    )(mq, mq, km, km, vm_t)


def _memkv_kernel(mem_ref, gmem_ref, w_ref, gxk_ref, mk_ref, mv_ref):
    kv = _dot(_rms(mem_ref[...], gmem_ref[...]).astype(BF16), w_ref[...])
    for h in range(X_HEADS):
        mk_ref[:, h * X_DH:(h + 1) * X_DH] = _rms(kv[:, h * X_DH:(h + 1) * X_DH], gxk_ref[...])
    mv_ref[...] = kv[:, X_W:]


def _memory_kv(mem, g_mem, w, g_xk):
    t, d = mem.shape
    tm = min(t, 512)
    out = jax.ShapeDtypeStruct((t, X_W), F32)
    ospec = pl.BlockSpec((tm, X_W), lambda i: (i, 0))
    return pl.pallas_call(
        _memkv_kernel,
        grid=(t // tm,),
        in_specs=[pl.BlockSpec((tm, d), lambda i: (i, 0)), pl.BlockSpec((1, d), lambda i: (0, 0)),
                  pl.BlockSpec(w.shape, lambda i: (0, 0)), pl.BlockSpec((1, X_DH), lambda i: (0, 0))],
        out_specs=[ospec, ospec],
        out_shape=[out, out],
        compiler_params=_cparams("parallel"),
        name="memory_kv",
    )(mem, g_mem, w, g_xk)


def _cross_kernel(q_ref, k_ref, v_ref, o_ref):
    for h in range(X_HEADS):
        sl = slice(h * X_DH, (h + 1) * X_DH)
        s = _dot_nt(q_ref[0, :, sl], k_ref[0, :, sl].astype(BF16)) * X_SCALE
        p = jnp.exp(s - jnp.max(s, axis=1, keepdims=True))
        p = p / jnp.sum(p, axis=1, keepdims=True)
        o_ref[0, :, sl] = _dot(p.astype(BF16), v_ref[0, :, sl].astype(BF16)).astype(BF16)


def _cross_prompt(xq, mk, mv):
    b, s, _ = xq.shape
    n_mem = mk.shape[1]
    tq = min(s, 512)
    qspec = pl.BlockSpec((1, tq, X_W), lambda bi, i: (bi, i, 0))
    kspec = pl.BlockSpec((1, n_mem, X_W), lambda bi, i: (bi, 0, 0))
    return pl.pallas_call(
        _cross_kernel,
        grid=(b, s // tq),
        in_specs=[qspec, kspec, kspec],
        out_specs=qspec,
        out_shape=jax.ShapeDtypeStruct((b, s, X_W), BF16),
        compiler_params=_cparams("parallel", "parallel"),
        name="cross_prompt",
    )(xq, mk, mv)


def _merge_kernel(u_ref, oa_ref, om_ref, ox_ref, h_ref, wg_ref, wpa_ref, wpm_ref, wpx_ref, wo_ref, y_ref):
    d = h_ref.shape[1]
    u = u_ref[...]
    comb = None
    for br, (o_ref, wp_ref) in enumerate(((oa_ref, wpa_ref), (om_ref, wpm_ref), (ox_ref, wpx_ref))):
        gate = jax.nn.sigmoid(_dot(u, wg_ref[:, br * d:(br + 1) * d]))
        term = gate * _dot(o_ref[...].astype(BF16), wp_ref[...])
        comb = term if comb is None else comb + term
    y_ref[...] = h_ref[...] + _dot(comb.astype(BF16), wo_ref[...])


def _merge(u, oa, om, ox, h, w):
    t, d = h.shape
    tm = min(t, 512)

    def tok(a):
        return pl.BlockSpec((tm, a.shape[1]), lambda i: (i, 0))

    def full(a):
        return pl.BlockSpec(a.shape, lambda i: (0, 0))

    consts = [w["w_gates"], w["w_pa"], w["w_pm"], w["w_px"], w["w_o"]]
    return pl.pallas_call(
        _merge_kernel,
        grid=(t // tm,),
        in_specs=[tok(u), tok(oa), tok(om), tok(ox), tok(h)] + [full(a) for a in consts],
        out_specs=tok(h),
        out_shape=jax.ShapeDtypeStruct((t, d), F32),
        compiler_params=_cparams("parallel"),
        name="merge",
    )(u, oa, om, ox, h, *consts)


def _fold_heads(res, head_mask, row_q, n_q):
    masked = res * head_mask
    return [jnp.sum(jnp.where(row_q == float(q), masked, 0.0), axis=0, keepdims=True) for q in range(n_q)]


def _paged_scores_kernel(pt_ref, *refs, npp, past_len):
    del pt_ref
    kp_refs = refs[:npp]
    ck_refs = refs[npp:2 * npp]
    kr_refs = refs[2 * npp:3 * npp]
    (qa_ref, qn_ref, qm_ref, ar_ref, ka_ref, km_ref, vm_ref, cos_ref, sin_ref, wkn_ref, wv_ref, gn_ref,
     hm5_ref, hm7_ref, rowc_ref, exp_ref,
     pa_ref, pown_ref, om_ref,
     sa_sc, sm_sc, ckv_sc, g_sc, aq_sc) = refs[3 * npp:]
    c = pl.program_id(1)
    nrow = sa_sc.shape[0]
    n_q = nrow // A_HEADS
    n_blocks = g_sc.shape[1]
    colg = lax.broadcasted_iota(jnp.int32, (nrow, n_blocks), 1)

    @pl.when(c == 0)
    def _():
        g_sc[...] = jnp.zeros_like(g_sc)
        qn = qn_ref[0].astype(F32) * gn_ref[...] * hm5_ref[...]
        aq_sc[...] = _dot(qn.astype(BF16), wkn_ref[...]).astype(BF16)

    qa = (qa_ref[0].astype(F32) * hm5_ref[...]).astype(BF16)
    aq = aq_sc[...]
    ar = ar_ref[0].astype(BF16)
    gates = g_sc[...]
    gkeys = GROUP_PAGES * PAGE_SIZE
    pages_per_block = A_BLOCK // PAGE_SIZE
    for gi in range(npp // GROUP_PAGES):
        pages = range(gi * GROUP_PAGES, (gi + 1) * GROUP_PAGES)
        first_page = c * npp + gi * GROUP_PAGES
        off = pl.multiple_of(first_page * PAGE_SIZE, gkeys)
        s_a = jnp.concatenate([_dot(qa, kp_refs[pi][0].astype(BF16)) for pi in pages], axis=1)
        sa_sc[:, pl.ds(off, gkeys)] = s_a
        for bi in range(GROUP_PAGES // pages_per_block):
            blk = first_page // pages_per_block + bi
            bsum = jnp.sum(s_a[:, bi * A_BLOCK:(bi + 1) * A_BLOCK], axis=1, keepdims=True)
            gates = gates + jnp.where(colg == blk, bsum, 0.0)
        ckb = jnp.concatenate([ck_refs[pi][0].astype(BF16) for pi in pages], axis=0)
        ckv_sc[pl.ds(off, gkeys), :] = ckb
        kn_t = _dot_nt(wkn_ref[...], ckb)
        ssq = jnp.sum((kn_t * kn_t).reshape(M_HEADS, M_NOPE, gkeys), axis=1)
        kr_t = jnp.concatenate([kr_refs[pi][0] for pi in pages], axis=1)
        ssr = jnp.sum(kr_t * kr_t, axis=0, keepdims=True)
        rinv = lax.rsqrt((ssq + ssr) * (1.0 / M_QK) + EPS)
        lsl = slice(gi * gkeys, (gi + 1) * gkeys)
        feat = jnp.concatenate([kr_t * cos_ref[:, lsl], kr_t * sin_ref[:, lsl]], axis=0).astype(BF16)
        s_m = _dot_nt(aq, ckb) + _dot(ar, feat)
        sm_sc[:, pl.ds(off, gkeys)] = s_m * jnp.concatenate([rinv] * n_q, axis=0)
    g_sc[...] = gates

    @pl.when(c == pl.num_programs(1) - 1)
    def _():
        slope = rowc_ref[:, 0:1]
        row_q = rowc_ref[:, 1:2]
        tpos = row_q + float(past_len)
        lc = min(LANE_CHUNK, past_len)
        n_ch = past_len // lc
        lane_new = lax.broadcasted_iota(jnp.int32, (nrow, LANES), 1).astype(F32)
        new_ok = jnp.logical_and(lane_new <= row_q, lane_new < float(n_q))

        sel = _top3_mask(g_sc[...], colg.astype(F32), n_blocks).astype(BF16)
        s_own = _dot_nt(qa, ka_ref[0]) * A_SCALE - slope * (row_q - lane_new)
        s_own = jnp.where(new_ok, s_own, NEG_INF)
        mx = jnp.max(s_own, axis=1, keepdims=True)
        for ch in range(n_ch):
            sl = slice(ch * lc, (ch + 1) * lc)
            kpos = (lax.broadcasted_iota(jnp.int32, (nrow, lc), 1) + ch * lc).astype(F32)
            s = sa_sc[:, sl] * A_SCALE - slope * (tpos - kpos)
            s = jnp.where(_dot(sel, exp_ref[:, sl]) > 0.5, s, NEG_INF)
            sa_sc[:, sl] = s
            mx = jnp.maximum(mx, jnp.max(s, axis=1, keepdims=True))
        p_own = jnp.exp(s_own - mx)
        den = jnp.sum(p_own, axis=1, keepdims=True)
        for ch in range(n_ch):
            sl = slice(ch * lc, (ch + 1) * lc)
            p = jnp.exp(sa_sc[:, sl] - mx)
            sa_sc[:, sl] = p
            den = den + jnp.sum(p, axis=1, keepdims=True)
        for ch in range(n_ch):
            sl = slice(ch * lc, (ch + 1) * lc)
            pa_ref[0, :, sl] = (sa_sc[:, sl] / den).astype(BF16)
        pown_ref[0] = p_own / den

        qm = (qm_ref[0].astype(F32) * hm7_ref[...]).astype(BF16)
        s_new = jnp.where(new_ok, _dot_nt(qm, km_ref[0]) * M_SCALE, NEG_INF)
        mx = jnp.max(s_new, axis=1, keepdims=True)
        for ch in range(n_ch):
            sl = slice(ch * lc, (ch + 1) * lc)
            s = sm_sc[:, sl] * M_SCALE
            sm_sc[:, sl] = s
            mx = jnp.maximum(mx, jnp.max(s, axis=1, keepdims=True))
        p_new = jnp.exp(s_new - mx)
        den = jnp.sum(p_new, axis=1, keepdims=True)
        for ch in range(n_ch):
            sl = slice(ch * lc, (ch + 1) * lc)
            p = jnp.exp(sm_sc[:, sl] - mx)
            sm_sc[:, sl] = p
            den = den + jnp.sum(p, axis=1, keepdims=True)
        pc = jnp.zeros((nrow, M_KV_RANK), F32)
        for ch in range(n_ch):
            sl = slice(ch * lc, (ch + 1) * lc)
            pc = pc + _dot((sm_sc[:, sl] / den).astype(BF16), ckv_sc[sl, :])
        pc_hi, pc_lo = _split_hi_lo(pc)
        res = (_dot(pc_hi, wv_ref[...]) + _dot(pc_lo, wv_ref[...])
               + _dot((p_new / den).astype(BF16), vm_ref[0]))
        rows = _fold_heads(res, hm5_ref[...], row_q, n_q)
        for q in range(n_q):
            om_ref[0, q:q + 1, :] = rows[q]


def _paged_values_kernel(pt_ref, *refs, npp):
    del pt_ref
    vp_refs = refs[:npp]
    pa_ref, pown_ref, va_ref, hm5_ref, rowc_ref, oa_ref, acc_sc = refs[npp:]
    c = pl.program_id(1)
    n_q = oa_ref.shape[1]

    @pl.when(c == 0)
    def _():
        acc_sc[...] = _dot(pown_ref[0].astype(BF16), va_ref[0])

    acc = acc_sc[...]
    for pi in range(npp):
        lsl = slice(pi * PAGE_SIZE, (pi + 1) * PAGE_SIZE)
        acc = acc + _dot_nt(pa_ref[0, :, lsl], vp_refs[pi][0].astype(BF16))
    acc_sc[...] = acc

    @pl.when(c == pl.num_programs(1) - 1)
    def _():
        rows = _fold_heads(acc, hm5_ref[...], rowc_ref[:, 1:2], n_q)
        for q in range(n_q):
            oa_ref[0, q:q + 1, :] = rows[q]


def _sample_mixers(page_table, cache_k, cache_v, cache_ckv, cache_kr, s, consts):
    nseq, n_pages = page_table.shape
    past_len = n_pages * PAGE_SIZE
    nrow = s["qa_rep"].shape[1]
    n_q = nrow // A_HEADS
    n_blocks = past_len // A_BLOCK

    def pages_per_step(want):
        return max(GROUP_PAGES, min(want, n_pages))

    def page_specs(rows, cols, npp):
        return [pl.BlockSpec((1, rows, cols), lambda b, c, pt, pi=pi: (pt[b, c * npp + pi], 0, 0))
                for pi in range(npp)]

    def seq_spec(a):
        return pl.BlockSpec((1,) + a.shape[1:], lambda b, c, pt: (b, 0, 0))

    def full(a):
        return pl.BlockSpec(a.shape, lambda b, c, pt: (0,) * a.ndim)

    npp = pages_per_step(SCORE_PAGES_PER_STEP)
    table = pl.BlockSpec((M_ROPE, npp * PAGE_SIZE), lambda b, c, pt: (0, c))
    seq_in = [s["qa_rep"], s["qn_rep"], s["qm_rep"], s["ar"], s["ka_new"], s["km_new"], s["vm_new"]]
    const_in = [consts["wkn_t"], consts["wv"], consts["g_nope"], consts["hm5"], consts["hm7"], consts["rowc"],
                consts["expand"]]
    pa, pown, om = pl.pallas_call(
        functools.partial(_paged_scores_kernel, npp=npp, past_len=past_len),
        grid_spec=pltpu.PrefetchScalarGridSpec(
            num_scalar_prefetch=1,
            grid=(nseq, n_pages // npp),
            in_specs=(page_specs(A_W, PAGE_SIZE, npp) + page_specs(PAGE_SIZE, M_KV_RANK, npp)
                      + page_specs(M_ROPE, PAGE_SIZE, npp)
                      + [seq_spec(a) for a in seq_in] + [table, table] + [full(a) for a in const_in]),
            out_specs=[pl.BlockSpec((1, nrow, past_len), lambda b, c, pt: (b, 0, 0)),
                       pl.BlockSpec((1, nrow, LANES), lambda b, c, pt: (b, 0, 0)),
                       pl.BlockSpec((1, n_q, M_W), lambda b, c, pt: (b, 0, 0))],
            scratch_shapes=[pltpu.VMEM((nrow, past_len), F32), pltpu.VMEM((nrow, past_len), F32),
                            pltpu.VMEM((past_len, M_KV_RANK), BF16), pltpu.VMEM((nrow, n_blocks), F32),
                            pltpu.VMEM((nrow, M_KV_RANK), BF16)]),
        out_shape=[jax.ShapeDtypeStruct((nseq, nrow, past_len), BF16),
                   jax.ShapeDtypeStruct((nseq, nrow, LANES), F32),
                   jax.ShapeDtypeStruct((nseq, n_q, M_W), F32)],
        compiler_params=_cparams("parallel", "arbitrary"),
        name="paged_scores",
    )(page_table, *([cache_k] * npp), *([cache_ckv] * npp), *([cache_kr] * npp), *seq_in,
      consts["cos_t"], consts["sin_t"], *const_in)

    npp = pages_per_step(VALUE_PAGES_PER_STEP)
    oa = pl.pallas_call(
        functools.partial(_paged_values_kernel, npp=npp),
        grid_spec=pltpu.PrefetchScalarGridSpec(
            num_scalar_prefetch=1,
            grid=(nseq, n_pages // npp),
            in_specs=(page_specs(A_W, PAGE_SIZE, npp)
                      + [pl.BlockSpec((1, nrow, npp * PAGE_SIZE), lambda b, c, pt: (b, 0, c)),
                         seq_spec(pown), seq_spec(s["va_new"]), full(consts["hm5"]), full(consts["rowc"])]),
            out_specs=pl.BlockSpec((1, n_q, A_W), lambda b, c, pt: (b, 0, 0)),
            scratch_shapes=[pltpu.VMEM((nrow, A_W), F32)]),
        out_shape=jax.ShapeDtypeStruct((nseq, n_q, A_W), F32),
        compiler_params=_cparams("parallel", "arbitrary"),
        name="paged_values",
    )(page_table, *([cache_v] * npp), pa, pown, s["va_new"], consts["hm5"], consts["rowc"])
    return oa, om


def _cross_sample_kernel(q_ref, k_ref, v_ref, o_ref):
    nrow = q_ref.shape[1]
    ncol = k_ref.shape[1]
    s = _dot_nt(q_ref[0], k_ref[0].astype(BF16)) * X_SCALE
    row_h = lax.broadcasted_iota(jnp.int32, (nrow, ncol), 0) % X_HEADS
    col_h = lax.broadcasted_iota(jnp.int32, (nrow, ncol), 1) % X_HEADS
    s = jnp.where(row_h == col_h, s, NEG_INF)
    p = jnp.exp(s - jnp.max(s, axis=1, keepdims=True))
    p = p / jnp.sum(p, axis=1, keepdims=True)
    o_ref[0] = _dot(p.astype(BF16), v_ref[0].astype(BF16))


def _cross_sample(xq_rows, mem_k, mem_v):
    nseq, nrow, _ = xq_rows.shape
    ncol = mem_k.shape[1]
    kspec = pl.BlockSpec((1, ncol, X_DH), lambda b: (b, 0, 0))
    qspec = pl.BlockSpec((1, nrow, X_DH), lambda b: (b, 0, 0))
    return pl.pallas_call(
        _cross_sample_kernel,
        grid=(nseq,),
        in_specs=[qspec, kspec, kspec],
        out_specs=qspec,
        out_shape=jax.ShapeDtypeStruct((nseq, nrow, X_DH), F32),
        compiler_params=_cparams("parallel"),
        name="cross_sample",
    )(xq_rows, mem_k, mem_v)


def _mla_perm():
    nope = [h * M_QK + d for h in range(M_HEADS) for d in range(M_NOPE)]
    lo = [h * M_QK + M_NOPE + j for h in range(M_HEADS) for j in range(M_HALF)]
    hi = [h * M_QK + M_NOPE + M_HALF + j for h in range(M_HEADS) for j in range(M_HALF)]
    return np.array(nope + lo + hi, np.int32)


def _static_consts():
    perm = _mla_perm()
    head7 = perm // M_QK
    dim7 = perm % M_QK
    e64 = (np.arange(A_W)[:, None] // A_DH == np.arange(A_W)[None, :] // A_DH)
    e96 = head7[:, None] == head7[None, :]
    tkr = np.zeros((M_ROPE, 2 * LANES), np.float32)
    for h in range(M_HEADS):
        for j in range(M_HALF):
            tkr[j, h * M_HALF + j] = 1.0
            tkr[M_HALF + j, LANES + h * M_HALF + j] = 1.0
    return perm, head7, dim7, e64.astype(np.float32), e96.astype(np.float32), tkr


def _rope_tables(pos):
    inv = ROPE_BASE ** (-jnp.arange(M_HALF, dtype=F32) / M_HALF)
    ang = pos.astype(F32)[:, None] * inv
    return jnp.cos(ang), jnp.sin(ang)


def _row_consts(n_q, heads, width, head_of_lane):
    r = np.arange(n_q * heads)
    hm = (head_of_lane[None, :] == (r % heads)[:, None]).astype(np.float32)
    rowc = np.zeros((n_q * heads, LANES), np.float32)
    rowc[:, 0] = 2.0 ** (-8.0 * ((r % heads) + 1) / heads)
    rowc[:, 1] = r // heads
    assert hm.shape[1] == width
    return hm, rowc


def kernel(x_prompt, mem_prompt, x_sample, cache_moba_k, cache_moba_v, cache_mla_ckv, cache_mla_kr, cache_mem_k, cache_mem_v, page_table, g_ffn1, w1_gate, w1_up, w1_down, g_mix, w_in, g_aq, g_ak, g_mqa, w_uq, g_mq, g_mkv, w_ukv, g_mk, g_mem, w_mem_kv, g_xq, g_xk, w_pa, w_pm, w_px, w_o, g_ffn2, w2_gate, w2_up, w2_down):
    depth = w_in.shape[0]
    assert depth == 1
    b, s, d = x_prompt.shape
    nseq, n_q, _ = x_sample.shape
    n_mem = mem_prompt.shape[1]
    n_pages = page_table.shape[1]
    past_len = n_pages * PAGE_SIZE
    assert past_len % A_BLOCK == 0 and past_len // A_BLOCK >= A_TOPK
    assert all(n_pages % min(p, n_pages) == 0 and min(p, n_pages) % GROUP_PAGES == 0
               for p in (SCORE_PAGES_PER_STEP, VALUE_PAGES_PER_STEP))
    assert s % 512 == 0 and (nseq * n_q) % A_BLOCK == 0
    l = 0

    perm, head7, dim7, e64, e96, tkr = _static_consts()

    def row(g):
        return g.reshape(1, -1).astype(F32)

    w_in_l = w_in[l]
    o = 3 * A_W + M_Q_RANK + M_KV_RANK
    kr_cols = w_in_l[:, o:o + M_ROPE]
    xq_cols = w_in_l[:, o + M_ROPE:o + M_ROPE + X_W]
    gate_cols = w_in_l[:, o + M_ROPE + X_W:]
    n_proj = o + X_W + M_ROPE
    pad = (-n_proj) % LANES
    wp = jnp.concatenate([w_in_l[:, :o], xq_cols, kr_cols, jnp.zeros((d, pad), F32)], axis=1).astype(BF16)
    w_ukv_l = w_ukv[l].reshape(M_KV_RANK, M_HEADS, M_NOPE + M_DV)
    wkn = w_ukv_l[:, :, :M_NOPE].reshape(M_KV_RANK, M_W)
    wv = w_ukv_l[:, :, M_NOPE:].reshape(M_KV_RANK, M_W)
    pw = {
        "wp": wp,
        "e64": jnp.asarray(e64, BF16),
        "e96": jnp.asarray(e96, BF16),
        "wuq": w_uq[l][:, perm].astype(BF16),
        "wukv": jnp.concatenate([wkn, wv], axis=1).astype(BF16),
        "tkr": jnp.asarray(tkr, BF16),
        "g_aq": row(jnp.tile(g_aq[l], A_HEADS)),
        "g_ak": row(jnp.tile(g_ak[l], A_HEADS)),
        "g_mqa": row(g_mqa[l]),
        "g_mq": row(g_mq[l][dim7]),
        "g_mkv": row(g_mkv[l]),
        "g_mk": row(g_mk[l][dim7]),
        "g_xq": row(g_xq[l]),
    }
    mw = {"w_gates": gate_cols.astype(BF16), "w_pa": w_pa[l].astype(BF16), "w_pm": w_pm[l].astype(BF16),
          "w_px": w_px[l].astype(BF16), "w_o": w_o[l].astype(BF16)}
    ffn1 = (row(g_ffn1[l]), w1_gate[l].astype(BF16), w1_up[l].astype(BF16), w1_down[l].astype(BF16), row(g_mix[l]))
    ffn2 = (row(g_ffn2[l]), w2_gate[l].astype(BF16), w2_up[l].astype(BF16), w2_down[l].astype(BF16), row(g_mix[l]))
    slopes = 2.0 ** (-8.0 * jnp.arange(1, A_HEADS + 1, dtype=F32) / A_HEADS)

    def lane_tables(pos):
        c, sn = _rope_tables(pos)
        return jnp.tile(c, (1, M_HEADS)), jnp.tile(sn, (1, M_HEADS))

    t = b * s
    hp, up = _ffn(x_prompt.reshape(t, d), *ffn1, with_u=True)
    cos_p, sin_p = lane_tables(jnp.arange(s))
    pr = _project(up, cos_p, sin_p, pw, tm=512, batch=b)
    mk, mv = _memory_kv(mem_prompt.reshape(b * n_mem, d), row(g_mem[l]), w_mem_kv[l].astype(BF16), row(g_xk[l]))

    def bsd(a):
        return a.reshape(b, s, a.shape[-1])

    oa = _moba_prompt(bsd(pr["aq"]), bsd(pr["ak_b"]), pr["av_b"],
                      pr["kmean"].reshape(b, s // A_BLOCK, A_W).astype(BF16), slopes)
    om = _mla_prompt(bsd(pr["mq"]), bsd(pr["km"]), pr["vm"])
    ox = _cross_prompt(bsd(pr["xq"]), mk.reshape(b, n_mem, X_W), mv.reshape(b, n_mem, X_W))
    h2 = _merge(up, oa.reshape(t, A_W), om.reshape(t, M_W), ox.reshape(t, X_W), hp, mw)
    yp = _ffn(h2, *ffn2, with_u=False).reshape(b, s, d)

    ts = nseq * n_q
    hs, us = _ffn(x_sample.reshape(ts, d), *ffn1, with_u=True)
    pos_s = past_len + (jnp.arange(ts) % n_q)
    cos_s, sin_s = lane_tables(pos_s)
    ps = _project(us, cos_s, sin_s, pw, tm=min(ts, 512))

    nrow = n_q * A_HEADS
    hm5, rowc = _row_consts(n_q, A_HEADS, A_W, np.arange(A_W) // A_DH)
    hm7, _ = _row_consts(n_q, M_HEADS, M_QW, np.where(np.arange(M_QW) < M_W, np.arange(M_QW) // M_NOPE,
                                                        (np.arange(M_QW) % LANES) // M_HALF))

    def rep_rows(a, heads):
        return jnp.repeat(a.reshape(nseq, n_q, a.shape[-1]), heads, axis=1)

    def pad_new(a):
        a = a.reshape(nseq, n_q, a.shape[-1])
        return jnp.pad(a, ((0, 0), (0, LANES - n_q), (0, 0)))

    mq_s = ps["mq"].astype(F32)
    q_lo = mq_s[:, A_W:A_W + LANES].reshape(nseq, n_q, M_HEADS, M_HALF)
    q_hi = mq_s[:, A_W + LANES:].reshape(nseq, n_q, M_HEADS, M_HALF)
    g_r = g_mk[l][M_NOPE:].astype(F32)
    ar_cos = jnp.concatenate([q_lo, q_hi], axis=-1) * g_r
    ar_sin = jnp.concatenate([q_hi, -q_lo], axis=-1) * g_r
    ar = jnp.concatenate([ar_cos, ar_sin], axis=-1).reshape(nseq, nrow, 2 * M_ROPE)

    c_all, s_all = _rope_tables(jnp.arange(past_len))
    n_blocks = past_len // A_BLOCK
    expand = (np.arange(past_len)[None, :] // A_BLOCK == np.arange(n_blocks)[:, None]).astype(np.float32)
    sc = {
        "wkn_t": wkn.T.astype(BF16),
        "wv": wv.astype(BF16),
        "g_nope": row(jnp.tile(g_mk[l][:M_NOPE], M_HEADS)),
        "hm5": jnp.asarray(hm5), "hm7": jnp.asarray(hm7), "rowc": jnp.asarray(rowc),
        "expand": jnp.asarray(expand, BF16),
        "cos_t": jnp.tile(c_all.T, (2, 1)), "sin_t": jnp.tile(s_all.T, (2, 1)),
    }
    sq = {
        "qa_rep": rep_rows(ps["aq"], A_HEADS),
        "qn_rep": rep_rows(ps["mq"][:, :M_W], M_HEADS),
        "qm_rep": rep_rows(ps["mq"], M_HEADS),
        "ar": ar,
        "ka_new": pad_new(ps["ak_b"]), "va_new": pad_new(ps["av_b"]),
        "km_new": pad_new(ps["km"]), "vm_new": pad_new(ps["vm"]),
    }
    n_pool = cache_moba_k.shape[1]

    def keys_on_lanes(c):
        c = c[l]
        return jnp.moveaxis(c, 1, -1).reshape(n_pool, -1, PAGE_SIZE)

    oa2, om2 = _sample_mixers(page_table, keys_on_lanes(cache_moba_k), keys_on_lanes(cache_moba_v),
                              cache_mla_ckv[l], keys_on_lanes(cache_mla_kr), sq, sc)
    ox2 = _cross_sample(ps["xq"].reshape(nseq, n_q * X_HEADS, X_DH),
                        cache_mem_k[l].reshape(nseq, n_mem * X_HEADS, X_DH),
                        cache_mem_v[l].reshape(nseq, n_mem * X_HEADS, X_DH))
    hs2 = _merge(us, oa2.reshape(ts, A_W), om2.reshape(ts, M_W), ox2.reshape(ts, X_W), hs, mw)
    ys = _ffn(hs2, *ffn2, with_u=False).reshape(nseq, n_q, d)

    def stack(a, shape):
        return a.reshape((1,) + shape)

    def tokens_back(a, shape):
        return jnp.moveaxis(a.reshape((b,) + shape + (s,)), -1, 1)[None]

    return (yp, ys,
            tokens_back(pr["ak"], (A_HEADS, A_DH)), tokens_back(pr["av"], (A_HEADS, A_DH)),
            stack(pr["ckv"], (b, s, M_KV_RANK)), tokens_back(pr["kr"], (M_ROPE,)),
            stack(mk, (b, n_mem, X_HEADS, X_DH)), stack(mv, (b, n_mem, X_HEADS, X_DH)),
            stack(ps["ak"], (nseq, n_q, A_HEADS, A_DH)), stack(ps["av"], (nseq, n_q, A_HEADS, A_DH)),
            stack(ps["ckv"], (nseq, n_q, M_KV_RANK)), stack(ps["kr"], (nseq, n_q, M_ROPE)))
```

```python
import functools

import numpy as np
import jax
import jax.numpy as jnp
from jax import lax
from jax.experimental import pallas as pl
from jax.experimental.pallas import tpu as pltpu

F32 = jnp.float32
BF16 = jnp.bfloat16

EPS = 1e-6
ROPE_BASE = 10000.0
PAGE_SIZE = 128
A_HEADS, A_DH, A_BLOCK, A_TOPK = 8, 64, 256, 3
A_W = A_HEADS * A_DH
A_SCALE = A_DH ** -0.5
M_HEADS, M_Q_RANK, M_KV_RANK, M_NOPE, M_ROPE, M_DV = 8, 256, 128, 64, 32, 64
M_QK = M_NOPE + M_ROPE
M_HALF = M_ROPE // 2
M_W = M_HEADS * M_DV
M_QW = M_HEADS * M_QK
M_SCALE = M_QK ** -0.5
X_HEADS, X_DH = 4, 128
X_W = X_HEADS * X_DH
X_SCALE = X_DH ** -0.5
N_BRANCH = 3
LANES = 128
VMEM_LIMIT = 56 * 1024 * 1024
NEG_INF = float("-inf")
SCORE_PAGES_PER_STEP = 32
GROUP_PAGES = 8
LANE_CHUNK = 2048

_NT = (((1,), (1,)), ((), ()))


def _dot(a, b):
    return jnp.dot(a, b, preferred_element_type=F32)


def _dot_nt(a, b):
    return lax.dot_general(a, b, _NT, preferred_element_type=F32)


def _cparams(*sem):
    return pltpu.CompilerParams(dimension_semantics=sem, vmem_limit_bytes=VMEM_LIMIT)


def _rms(x, g):
    return x * lax.rsqrt(jnp.mean(x * x, axis=-1, keepdims=True) + EPS) * g


def _split_hi_lo(x):
    hi = x.astype(BF16)
    lo = (x - hi.astype(F32)).astype(BF16)
    return hi, lo


def _top3_mask(g, idx, n, axis=1):
    sel, _ = _top3(g, idx, n, axis)
    return sel


def _top3(g, idx, n, axis=1):
    sel = jnp.zeros_like(g)
    chosen = []
    for _ in range(A_TOPK):
        m = jnp.max(g, axis=axis, keepdims=True)
        first = jnp.min(jnp.where(g == m, idx, float(n)), axis=axis, keepdims=True)
        pick = idx == first
        sel = jnp.where(jnp.logical_and(pick, m > NEG_INF), 1.0, sel)
        g = jnp.where(pick, NEG_INF, g)
        chosen.append(first)
    return sel, chosen


def _ffn_kernel(x_ref, g_ref, wg_ref, wu_ref, wd_ref, g2_ref, *rest, with_u):
    if with_u:
        y_ref, u_ref, xn_sc, acc_sc = rest
    else:
        y_ref, xn_sc, acc_sc = rest
    k = pl.program_id(1)

    @pl.when(k == 0)
    def _():
        xn_sc[...] = _rms(x_ref[...], g_ref[...]).astype(BF16)
        acc_sc[...] = jnp.zeros_like(acc_sc)

    xn = xn_sc[...]
    gate = _dot(xn, wg_ref[...])
    up = _dot(xn, wu_ref[...])
    hid = gate * jax.nn.sigmoid(gate) * up
    acc_sc[...] += _dot(hid.astype(BF16), wd_ref[...])

    @pl.when(k == pl.num_programs(1) - 1)
    def _():
        y = x_ref[...] + 0.5 * acc_sc[...]
        y_ref[...] = y
        if with_u:
            u_ref[...] = _rms(y, g2_ref[...]).astype(BF16)


def _ffn(x, g, wg, wu, wd, g2, with_u):
    t, d = x.shape
    dff = wg.shape[1]
    tm = min(t, 512)
    tf = dff // 2 if (dff // 2) % LANES == 0 else dff
    grid = (t // tm, dff // tf)
    tok = pl.BlockSpec((tm, d), lambda i, k: (i, 0))
    vec = pl.BlockSpec((1, d), lambda i, k: (0, 0))
    out_shape = [jax.ShapeDtypeStruct((t, d), F32)]
    out_specs = [tok]
    if with_u:
        out_shape.append(jax.ShapeDtypeStruct((t, d), BF16))
        out_specs.append(tok)
    res = pl.pallas_call(
        functools.partial(_ffn_kernel, with_u=with_u),
        grid=grid,
        in_specs=[tok, vec,
                  pl.BlockSpec((d, tf), lambda i, k: (0, k)),
                  pl.BlockSpec((d, tf), lambda i, k: (0, k)),
                  pl.BlockSpec((tf, d), lambda i, k: (k, 0)),
                  vec],
        out_specs=out_specs,
        out_shape=out_shape,
        scratch_shapes=[pltpu.VMEM((tm, d), BF16), pltpu.VMEM((tm, d), F32)],
        compiler_params=_cparams("parallel", "arbitrary"),
        name="ffn",
    )(x, g, wg, wu, wd, g2)
    return res if with_u else res[0]


def _project_kernel(u_ref, cos_ref, sin_ref, wp_ref, e64_ref, e96_ref, wuq_ref, wukv_ref, tkr_ref,
                    gaq_ref, gak_ref, gmqa_ref, gmq_ref, gmkv_ref, gmk_ref, gxq_ref,
                    aq_ref, akf_ref, akb_ref, avf_ref, avb_ref, mq_ref, ckv_ref, kr_ref, xq_ref,
                    km_ref, vm_ref, kmean_ref, *, tokens_on_lanes):
    tm = u_ref.shape[0]

    def store_rows(x, *out_refs):
        if tokens_on_lanes:
            x = x.T
        for ref in out_refs:
            if tokens_on_lanes:
                ref[0] = x[:ref.shape[1]].astype(ref.dtype)
            else:
                ref[...] = x[:, :ref.shape[1]].astype(ref.dtype)

    z = _dot(u_ref[...], wp_ref[...])
    cs = cos_ref[...]
    sn = sin_ref[...]

    def group_norm(x, e_ref, n):
        ss = _dot((x * x).astype(BF16), e_ref[...])
        return x * lax.rsqrt(ss * (1.0 / n) + EPS)

    def rope_store(xn, out_ref):
        x1 = xn[:, A_W:A_W + LANES]
        x2 = xn[:, A_W + LANES:]
        out_ref[:, :A_W] = xn[:, :A_W].astype(BF16)
        out_ref[:, A_W:A_W + LANES] = (x1 * cs - x2 * sn).astype(BF16)
        out_ref[:, A_W + LANES:] = (x1 * sn + x2 * cs).astype(BF16)

    aq = group_norm(z[:, 0:A_W], e64_ref, A_DH) * gaq_ref[...]
    aq_ref[...] = aq.astype(BF16)
    ak = group_norm(z[:, A_W:2 * A_W], e64_ref, A_DH) * gak_ref[...]
    store_rows(ak, akf_ref)
    akb_ref[...] = ak.astype(BF16)
    for blk in range(tm // A_BLOCK):
        kmean_ref[blk] = jnp.mean(ak[blk * A_BLOCK:(blk + 1) * A_BLOCK], axis=0, keepdims=True)
    av = z[:, 2 * A_W:3 * A_W]
    store_rows(av, avf_ref, avb_ref)

    o = 3 * A_W
    mqa = _rms(z[:, o:o + M_Q_RANK], gmqa_ref[...])
    mq = _dot(mqa.astype(BF16), wuq_ref[...])
    rope_store(group_norm(mq, e96_ref, M_QK) * gmq_ref[...], mq_ref)

    o += M_Q_RANK
    ckv = _rms(z[:, o:o + M_KV_RANK], gmkv_ref[...])
    ckv_ref[...] = ckv
    o += M_KV_RANK
    xq = z[:, o:o + X_W]
    o += X_W
    kr = z[:, o:o + M_ROPE]
    store_rows(z[:, o:o + LANES], kr_ref)
    kv = _dot(ckv.astype(BF16), wukv_ref[...])
    store_rows(kv[:, M_W:], vm_ref)
    kr_hi, kr_lo = _split_hi_lo(kr)
    kr_t = _dot(kr_hi, tkr_ref[...]) + _dot(kr_lo, tkr_ref[...])
    kpre = jnp.concatenate([kv[:, :M_W], kr_t], axis=1)
    rope_store(group_norm(kpre, e96_ref, M_QK) * gmk_ref[...], km_ref)

    for h in range(X_HEADS):
        xh = xq[:, h * X_DH:(h + 1) * X_DH]
        xq_ref[:, h * X_DH:(h + 1) * X_DH] = _rms(xh, gxq_ref[...]).astype(BF16)


def _project(u, cos_t, sin_t, w, tm, batch=None):
    t = u.shape[0]
    n_rope_blocks = cos_t.shape[0] // tm
    grid = (t // tm,)
    nb = tm // A_BLOCK
    cache_rows = ("ak", "av", "av_b", "kr", "vm")

    def tok(width):
        return pl.BlockSpec((tm, width), lambda i: (i, 0))

    def out_shape_spec(name, width, dt):
        if batch is None or name not in cache_rows:
            return jax.ShapeDtypeStruct((t, width), dt), tok(width)
        per_seq = t // batch // tm
        return (jax.ShapeDtypeStruct((batch, width, t // batch), dt),
                pl.BlockSpec((1, width, tm), lambda i: (i // per_seq, 0, i % per_seq)))

    def full(a):
        return pl.BlockSpec(a.shape, lambda i: (0,) * a.ndim)

    rope = pl.BlockSpec((tm, LANES), lambda i: (i % n_rope_blocks, 0))
    consts = [w["wp"], w["e64"], w["e96"], w["wuq"], w["wukv"], w["tkr"],
              w["g_aq"], w["g_ak"], w["g_mqa"], w["g_mq"], w["g_mkv"], w["g_mk"], w["g_xq"]]
    outs = [("aq", A_W, BF16), ("ak", A_W, F32), ("ak_b", A_W, BF16), ("av", A_W, F32), ("av_b", A_W, BF16),
            ("mq", M_QW, BF16), ("ckv", M_KV_RANK, F32), ("kr", M_ROPE, F32), ("xq", X_W, BF16),
            ("km", M_QW, BF16), ("vm", M_W, BF16)]
    out_shape, out_specs = (list(x) for x in zip(*(out_shape_spec(*o) for o in outs)))
    out_shape.append(jax.ShapeDtypeStruct((t // A_BLOCK, 1, A_W), F32))
    out_specs.append(pl.BlockSpec((nb, 1, A_W), lambda i: (i, 0, 0)))
    res = pl.pallas_call(
        functools.partial(_project_kernel, tokens_on_lanes=batch is not None),
        grid=grid,
        in_specs=[tok(u.shape[1]), rope, rope] + [full(a) for a in consts],
        out_specs=out_specs,
        out_shape=out_shape,
        compiler_params=_cparams("parallel"),
        name="project",
    )(u, cos_t, sin_t, *consts)
    named = {name: r for (name, _, _), r in zip(outs, res[:-1])}
    named["kmean"] = res[-1]
    return named


def _attend_blocks(n_blocks, tq, score_block, v_t_ref, rows, s_sc, p_sc):
    m = None
    for b in range(n_blocks):
        s_t = score_block(b)
        s_sc[b * tq:(b + 1) * tq, :] = s_t
        bm = jnp.max(s_t, axis=0, keepdims=True)
        m = bm if m is None else jnp.maximum(m, bm)
    l = None
    for b in range(n_blocks):
        p = jnp.exp(s_sc[b * tq:(b + 1) * tq, :] - m)
        p_sc[b * tq:(b + 1) * tq, :] = p.astype(BF16)
        bl = jnp.sum(p, axis=0, keepdims=True)
        l = bl if l is None else l + bl
    nk = n_blocks * tq
    return _dot(v_t_ref[0, rows, 0:nk], p_sc[0:nk, :]) / l


def _per_tile_index(i, n_tiles, body):
    for ii in range(n_tiles):
        @pl.when(i == ii)
        def _(ii=ii):
            body(ii)


def _head_rows(e, dv):
    return slice(e * dv, (e + 1) * dv)


def _moba_kernel(slope_ref, q_ref, k_ref, vt_ref, km_ref, o_ref, s_sc, p_sc):
    tq = q_ref.shape[1]
    pair = pl.program_id(1)
    i = pl.program_id(2)
    nblk = km_ref.shape[1]
    qf = q_ref[0].astype(F32) * A_SCALE
    lane = lax.broadcasted_iota(jnp.int32, (tq, LANES), 1)
    rel_t = (lax.broadcasted_iota(jnp.int32, (tq, tq), 1)
             - lax.broadcasted_iota(jnp.int32, (tq, tq), 0)).astype(F32)
    rowb = lax.broadcasted_iota(jnp.int32, (nblk, tq), 0).astype(F32)
    i_f = i.astype(F32)
    qs, sels, slopes = [], [], []
    for e in range(2):
        in_head = (lane < A_DH) if e == 0 else (lane >= A_DH)
        qe = jnp.where(in_head, qf, 0.0).astype(BF16)
        gate_t = _dot_nt(km_ref[0], qe)
        sels.append(_top3_mask(jnp.where(rowb < i_f, gate_t, NEG_INF), rowb, nblk, axis=0))
        qs.append(qe)
        slopes.append(slope_ref[2 * pair + e])

    def body(ii):
        nk = (ii + 1) * tq
        outs = []
        for e in range(2):
            s_all = _dot_nt(k_ref[0, 0:nk, :], qs[e])
            bias = slopes[e] * rel_t

            def score_block(b, e=e, s_all=s_all, bias=bias):
                s_t = s_all[b * tq:(b + 1) * tq] - bias - slopes[e] * float((ii - b) * tq)
                keep = (rel_t >= 0.0) if b == ii else (sels[e][b:b + 1] > 0.5)
                return jnp.where(keep, s_t, NEG_INF)

            outs.append(_attend_blocks(ii + 1, tq, score_block, vt_ref, _head_rows(e, A_DH),
                                       s_sc.at[e], p_sc.at[e]))
        o_ref[0] = jnp.concatenate(outs, axis=0).T.astype(BF16)

    _per_tile_index(i, nblk, body)


def _attn_scratch(s, tq):
    return [pltpu.VMEM((2, s, tq), F32), pltpu.VMEM((2, s, tq), BF16)]


def _moba_prompt(aq, ak, av_t, kmean, slopes):
    b, s, _ = aq.shape
    tq = A_BLOCK
    nblk = s // A_BLOCK
    qspec = pl.BlockSpec((1, tq, LANES), lambda bi, p, i: (bi, i, p))
    return pl.pallas_call(
        _moba_kernel,
        grid=(b, A_W // LANES, s // tq),
        in_specs=[pl.BlockSpec(memory_space=pltpu.SMEM), qspec,
                  pl.BlockSpec((1, s, LANES), lambda bi, p, i: (bi, 0, p)),
                  pl.BlockSpec((1, LANES, s), lambda bi, p, i: (bi, p, 0)),
                  pl.BlockSpec((1, nblk, LANES), lambda bi, p, i: (bi, 0, p))],
        out_specs=qspec,
        out_shape=jax.ShapeDtypeStruct((b, s, A_W), BF16),
        scratch_shapes=_attn_scratch(s, tq),
        compiler_params=_cparams("parallel", "parallel", "arbitrary"),
        name="moba_prompt",
    )(slopes, aq, ak, av_t, kmean)


def _mla_kernel(qn_ref, qr_ref, kn_ref, kr_ref, vt_ref, o_ref, s_sc, p_sc):
    tq = qn_ref.shape[1]
    pair = pl.program_id(1)
    i = pl.program_id(2)
    n_tiles = kn_ref.shape[1] // tq
    qn = qn_ref[0].astype(F32)
    qr = qr_ref[0].astype(F32)
    lane = lax.broadcasted_iota(jnp.int32, (tq, LANES), 1)
    lane_r = lax.broadcasted_iota(jnp.int32, (tq, 2 * LANES), 1) % LANES
    rel_t = (lax.broadcasted_iota(jnp.int32, (tq, tq), 1)
             - lax.broadcasted_iota(jnp.int32, (tq, tq), 0))
    qs = []
    for e in range(2):
        head = 2 * pair + e
        in_head = (lane < M_NOPE) if e == 0 else (lane >= M_NOPE)
        in_rope = jnp.logical_and(lane_r >= head * M_HALF, lane_r < (head + 1) * M_HALF)
        qs.append([jnp.where(in_head, qn, 0.0).astype(BF16), jnp.where(in_rope, qr, 0.0).astype(BF16)])

    def body(ii):
        nk = (ii + 1) * tq
        outs = []
        for e in range(2):
            s_all = (_dot_nt(kn_ref[0, 0:nk, :], qs[e][0]) + _dot_nt(kr_ref[0, 0:nk, :], qs[e][1])) * M_SCALE

            def score_block(b, s_all=s_all):
                s_t = s_all[b * tq:(b + 1) * tq]
                return jnp.where(rel_t >= 0, s_t, NEG_INF) if b == ii else s_t

            outs.append(_attend_blocks(ii + 1, tq, score_block, vt_ref, _head_rows(e, M_DV),
                                       s_sc.at[e], p_sc.at[e]))
        o_ref[0] = jnp.concatenate(outs, axis=0).T.astype(BF16)

    _per_tile_index(i, n_tiles, body)


def _mla_prompt(mq, km, vm_t):
    b, s, _ = mq.shape
    tq = A_BLOCK
    rope_blk = M_W // (2 * LANES)
    qspec = pl.BlockSpec((1, tq, LANES), lambda bi, p, i: (bi, i, p))
    return pl.pallas_call(
        _mla_kernel,
        grid=(b, M_W // LANES, s // tq),
        in_specs=[qspec, pl.BlockSpec((1, tq, 2 * LANES), lambda bi, p, i: (bi, i, rope_blk)),
                  pl.BlockSpec((1, s, LANES), lambda bi, p, i: (bi, 0, p)),
                  pl.BlockSpec((1, s, 2 * LANES), lambda bi, p, i: (bi, 0, rope_blk)),
                  pl.BlockSpec((1, LANES, s), lambda bi, p, i: (bi, p, 0))],
        out_specs=qspec,
        out_shape=jax.ShapeDtypeStruct((b, s, M_W), BF16),
        scratch_shapes=_attn_scratch(s, tq),
        compiler_params=_cparams("parallel", "parallel", "arbitrary"),
        name="mla_prompt",
    )(mq, mq, km, km, vm_t)


def _memkv_kernel(mem_ref, gmem_ref, w_ref, gxk_ref, mk_ref, mv_ref):
    kv = _dot(_rms(mem_ref[...], gmem_ref[...]).astype(BF16), w_ref[...])
    for h in range(X_HEADS):
        mk_ref[:, h * X_DH:(h + 1) * X_DH] = _rms(kv[:, h * X_DH:(h + 1) * X_DH], gxk_ref[...])
    mv_ref[...] = kv[:, X_W:]


def _memory_kv(mem, g_mem, w, g_xk):
    t, d = mem.shape
    tm = min(t, 512)
    out = jax.ShapeDtypeStruct((t, X_W), F32)
    ospec = pl.BlockSpec((tm, X_W), lambda i: (i, 0))
    return pl.pallas_call(
        _memkv_kernel,
        grid=(t // tm,),
        in_specs=[pl.BlockSpec((tm, d), lambda i: (i, 0)), pl.BlockSpec((1, d), lambda i: (0, 0)),
                  pl.BlockSpec(w.shape, lambda i: (0, 0)), pl.BlockSpec((1, X_DH), lambda i: (0, 0))],
        out_specs=[ospec, ospec],
        out_shape=[out, out],
        compiler_params=_cparams("parallel"),
        name="memory_kv",
    )(mem, g_mem, w, g_xk)


def _cross_kernel(q_ref, k_ref, v_ref, o_ref):
    for h in range(X_HEADS):
        sl = slice(h * X_DH, (h + 1) * X_DH)
        s = _dot_nt(q_ref[0, :, sl], k_ref[0, :, sl].astype(BF16)) * X_SCALE
        p = jnp.exp(s - jnp.max(s, axis=1, keepdims=True))
        p = p / jnp.sum(p, axis=1, keepdims=True)
        o_ref[0, :, sl] = _dot(p.astype(BF16), v_ref[0, :, sl].astype(BF16)).astype(BF16)


def _cross_prompt(xq, mk, mv):
    b, s, _ = xq.shape
    n_mem = mk.shape[1]
    tq = min(s, 512)
    qspec = pl.BlockSpec((1, tq, X_W), lambda bi, i: (bi, i, 0))
    kspec = pl.BlockSpec((1, n_mem, X_W), lambda bi, i: (bi, 0, 0))
    return pl.pallas_call(
        _cross_kernel,
        grid=(b, s // tq),
        in_specs=[qspec, kspec, kspec],
        out_specs=qspec,
        out_shape=jax.ShapeDtypeStruct((b, s, X_W), BF16),
        compiler_params=_cparams("parallel", "parallel"),
        name="cross_prompt",
    )(xq, mk, mv)


def _merge_kernel(u_ref, oa_ref, om_ref, ox_ref, h_ref, wg_ref, wpa_ref, wpm_ref, wpx_ref, wo_ref, y_ref):
    d = h_ref.shape[1]
    u = u_ref[...]
    comb = None
    for br, (o_ref, wp_ref) in enumerate(((oa_ref, wpa_ref), (om_ref, wpm_ref), (ox_ref, wpx_ref))):
        gate = jax.nn.sigmoid(_dot(u, wg_ref[:, br * d:(br + 1) * d]))
        term = gate * _dot(o_ref[...].astype(BF16), wp_ref[...])
        comb = term if comb is None else comb + term
    y_ref[...] = h_ref[...] + _dot(comb.astype(BF16), wo_ref[...])


def _merge(u, oa, om, ox, h, w):
    t, d = h.shape
    tm = min(t, 512)

    def tok(a):
        return pl.BlockSpec((tm, a.shape[1]), lambda i: (i, 0))

    def full(a):
        return pl.BlockSpec(a.shape, lambda i: (0, 0))

    consts = [w["w_gates"], w["w_pa"], w["w_pm"], w["w_px"], w["w_o"]]
    return pl.pallas_call(
        _merge_kernel,
        grid=(t // tm,),
        in_specs=[tok(u), tok(oa), tok(om), tok(ox), tok(h)] + [full(a) for a in consts],
        out_specs=tok(h),
        out_shape=jax.ShapeDtypeStruct((t, d), F32),
        compiler_params=_cparams("parallel"),
        name="merge",
    )(u, oa, om, ox, h, *consts)


def _fold_heads(res, head_mask, row_q, n_q):
    masked = res * head_mask
    return [jnp.sum(jnp.where(row_q == float(q), masked, 0.0), axis=0, keepdims=True) for q in range(n_q)]


def _paged_scores_kernel(pt_ref, *refs, npp, past_len):
    del pt_ref
    kp_refs = refs[:npp]
    ck_refs = refs[npp:2 * npp]
    kr_refs = refs[2 * npp:3 * npp]
    (qa_ref, qn_ref, qm_ref, ar_ref, ka_ref, va_ref, km_ref, vm_ref, cos_ref, sin_ref, wkn_ref, wv_ref, gn_ref,
     hm5_ref, hm7_ref, rowc_ref, exp_ref,
     pa_ref, top_ref, den_ref, oa_own_ref, om_ref,
     sa_sc, sm_sc, ckv_sc, g_sc, aq_sc, mx_sc) = refs[3 * npp:]
    c = pl.program_id(1)
    nrow = sa_sc.shape[0]
    n_q = nrow // A_HEADS
    n_blocks = g_sc.shape[1]
    colg = lax.broadcasted_iota(jnp.int32, (nrow, n_blocks), 1)

    @pl.when(c == 0)
    def _():
        g_sc[...] = jnp.zeros_like(g_sc)
        mx_sc[...] = jnp.full_like(mx_sc, NEG_INF)
        qn = qn_ref[0].astype(F32) * gn_ref[...] * hm5_ref[...]
        aq_sc[...] = (_dot(qn.astype(BF16), wkn_ref[...]) * M_SCALE).astype(BF16)

    qa = (qa_ref[0].astype(F32) * hm5_ref[...] * A_SCALE).astype(BF16)
    aq = aq_sc[...]
    ar = ar_ref[0].astype(BF16)
    gates = g_sc[...]
    m_run = mx_sc[:, 0:1]
    gkeys = GROUP_PAGES * PAGE_SIZE
    pages_per_block = A_BLOCK // PAGE_SIZE
    for gi in range(npp // GROUP_PAGES):
        pages = range(gi * GROUP_PAGES, (gi + 1) * GROUP_PAGES)
        first_page = c * npp + gi * GROUP_PAGES
        off = pl.multiple_of(first_page * PAGE_SIZE, gkeys)
        s_a = jnp.concatenate([_dot(qa, kp_refs[pi][0].astype(BF16)) for pi in pages], axis=1)
        sa_sc[:, pl.ds(off, gkeys)] = s_a
        for bi in range(GROUP_PAGES // pages_per_block):
            blk = first_page // pages_per_block + bi
            bsum = jnp.sum(s_a[:, bi * A_BLOCK:(bi + 1) * A_BLOCK], axis=1, keepdims=True)
            gates = gates + jnp.where(colg == blk, bsum, 0.0)
        ckb = jnp.concatenate([ck_refs[pi][0].astype(BF16) for pi in pages], axis=0)
        ckv_sc[pl.ds(off, gkeys), :] = ckb
        kn_t = _dot_nt(wkn_ref[...], ckb)
        ssq = jnp.sum((kn_t * kn_t).reshape(M_HEADS, M_NOPE, gkeys), axis=1)
        kr_t = jnp.concatenate([kr_refs[pi][0] for pi in pages], axis=1)
        ssr = jnp.sum(kr_t * kr_t, axis=0, keepdims=True)
        rinv = lax.rsqrt((ssq + ssr) * (1.0 / M_QK) + EPS)
        lsl = slice(gi * gkeys, (gi + 1) * gkeys)
        feat = jnp.concatenate([kr_t * cos_ref[:, lsl], kr_t * sin_ref[:, lsl]], axis=0).astype(BF16)
        s_m = (_dot_nt(aq, ckb) + _dot(ar, feat)) * jnp.concatenate([rinv] * n_q, axis=0)
        sm_sc[:, pl.ds(off, gkeys)] = s_m
        m_run = jnp.maximum(m_run, jnp.max(s_m, axis=1, keepdims=True))
    g_sc[...] = gates
    mx_sc[...] = jnp.broadcast_to(m_run, mx_sc.shape)

    @pl.when(c == pl.num_programs(1) - 1)
    def _():
        slope = rowc_ref[:, 0:1]
        row_q = rowc_ref[:, 1:2]
        tpos = row_q + float(past_len)
        lc = min(LANE_CHUNK, past_len)
        n_ch = past_len // lc
        lane_i = lax.broadcasted_iota(jnp.int32, (nrow, LANES), 1)
        lane_new = lane_i.astype(F32)
        new_ok = jnp.logical_and(lane_new <= row_q, lane_new < float(n_q))

        sel, chosen = _top3(gates, colg.astype(F32), n_blocks)
        sel = sel.astype(BF16)
        top = jnp.zeros((nrow, LANES), F32)
        for k, idx in enumerate(chosen):
            top = jnp.where(lane_i == k, idx, top)
        top_ref[0] = top.astype(jnp.int32)
        s_own = _dot_nt(qa, ka_ref[0]) - slope * (row_q - lane_new)
        s_own = jnp.where(new_ok, s_own, NEG_INF)
        mx = jnp.max(s_own, axis=1, keepdims=True)
        for ch in range(n_ch):
            sl = slice(ch * lc, (ch + 1) * lc)
            kpos = (lax.broadcasted_iota(jnp.int32, (nrow, lc), 1) + ch * lc).astype(F32)
            s = sa_sc[:, sl] - slope * (tpos - kpos)
            s = jnp.where(_dot(sel, exp_ref[:, sl]) > 0.5, s, NEG_INF)
            sa_sc[:, sl] = s
            mx = jnp.maximum(mx, jnp.max(s, axis=1, keepdims=True))
        p_own = jnp.exp(s_own - mx)
        den = jnp.sum(p_own, axis=1, keepdims=True)
        for ch in range(n_ch):
            sl = slice(ch * lc, (ch + 1) * lc)
            p = jnp.exp(sa_sc[:, sl] - mx)
            pa_ref[0, :, sl] = p
            den = den + jnp.sum(p, axis=1, keepdims=True)
        den_ref[0] = jnp.broadcast_to(den, (nrow, LANES))
        rows = _fold_heads(_dot(p_own.astype(BF16), va_ref[0]) / den, hm5_ref[...], row_q, n_q)
        for q in range(n_q):
            oa_own_ref[0, q:q + 1, :] = rows[q]

        qm = (qm_ref[0].astype(F32) * hm7_ref[...]).astype(BF16)
        s_new = jnp.where(new_ok, _dot_nt(qm, km_ref[0]) * M_SCALE, NEG_INF)
        mx = jnp.maximum(m_run, jnp.max(s_new, axis=1, keepdims=True))
        p_new = jnp.exp(s_new - mx)
        den = jnp.sum(p_new, axis=1, keepdims=True)
        pc = jnp.zeros((nrow, M_KV_RANK), F32)
        for ch in range(n_ch):
            sl = slice(ch * lc, (ch + 1) * lc)
            p = jnp.exp(sm_sc[:, sl] - mx)
            den = den + jnp.sum(p, axis=1, keepdims=True)
            pc = pc + _dot(p.astype(BF16), ckv_sc[sl, :])
        pc_hi, pc_lo = _split_hi_lo(pc)
        res = (_dot(pc_hi, wv_ref[...]) + _dot(pc_lo, wv_ref[...]) + _dot(p_new.astype(BF16), vm_ref[0])) / den
        rows = _fold_heads(res, hm5_ref[...], row_q, n_q)
        for q in range(n_q):
            om_ref[0, q:q + 1, :] = rows[q]


def _selected_values_kernel(pt_ref, top_ref, *refs, nrow):
    del pt_ref, top_ref
    n_slot = nrow * A_TOPK
    pages_per_block = A_BLOCK // PAGE_SIZE
    p_refs = refs[:n_slot]
    v_refs = refs[n_slot:n_slot * (1 + pages_per_block)]
    den_ref, o_ref = refs[n_slot * (1 + pages_per_block):]
    for r in range(nrow):
        slots = range(r * A_TOPK, (r + 1) * A_TOPK)
        p = jnp.concatenate([p_refs[t][0, 0] for t in slots], axis=1)
        v_t = jnp.concatenate([v_refs[t * pages_per_block + g][0] for t in slots for g in range(pages_per_block)],
                              axis=1)
        res = _dot_nt(jnp.broadcast_to(p, (8, p.shape[1])).astype(BF16), v_t.astype(BF16))
        o_ref[0, r:r + 1, :] = res[0:1] / den_ref[0, r:r + 1, 0:A_DH]


def _sample_mixers(page_table, cache_k, cache_v, cache_ckv, cache_kr, s, consts):
    nseq, n_pages = page_table.shape
    past_len = n_pages * PAGE_SIZE
    nrow = s["qa_rep"].shape[1]
    n_q = nrow // A_HEADS
    n_blocks = past_len // A_BLOCK

    def pages_per_step(want):
        return max(GROUP_PAGES, min(want, n_pages))

    def page_specs(rows, cols, npp):
        return [pl.BlockSpec((1, rows, cols), lambda b, c, pt, pi=pi: (pt[b, c * npp + pi], 0, 0))
                for pi in range(npp)]

    def seq_spec(a):
        return pl.BlockSpec((1,) + a.shape[1:], lambda b, c, pt: (b, 0, 0))

    def full(a):
        return pl.BlockSpec(a.shape, lambda b, c, pt: (0,) * a.ndim)

    npp = pages_per_step(SCORE_PAGES_PER_STEP)
    table = pl.BlockSpec((M_ROPE, npp * PAGE_SIZE), lambda b, c, pt: (0, c))
    seq_in = [s["qa_rep"], s["qn_rep"], s["qm_rep"], s["ar"], s["ka_new"], s["va_new"], s["km_new"], s["vm_new"]]
    const_in = [consts["wkn_t"], consts["wv"], consts["g_nope"], consts["hm5"], consts["hm7"], consts["rowc"],
                consts["expand"]]
    row_block = pl.BlockSpec((1, nrow, LANES), lambda b, c, pt: (b, 0, 0))
    pa, top, den, oa_own, om = pl.pallas_call(
        functools.partial(_paged_scores_kernel, npp=npp, past_len=past_len),
        grid_spec=pltpu.PrefetchScalarGridSpec(
            num_scalar_prefetch=1,
            grid=(nseq, n_pages // npp),
            in_specs=(page_specs(A_W, PAGE_SIZE, npp) + page_specs(PAGE_SIZE, M_KV_RANK, npp)
                      + page_specs(M_ROPE, PAGE_SIZE, npp)
                      + [seq_spec(a) for a in seq_in] + [table, table] + [full(a) for a in const_in]),
            out_specs=[pl.BlockSpec((1, nrow, past_len), lambda b, c, pt: (b, 0, 0)),
                       row_block, row_block,
                       pl.BlockSpec((1, n_q, A_W), lambda b, c, pt: (b, 0, 0)),
                       pl.BlockSpec((1, n_q, M_W), lambda b, c, pt: (b, 0, 0))],
            scratch_shapes=[pltpu.VMEM((nrow, past_len), F32), pltpu.VMEM((nrow, past_len), F32),
                            pltpu.VMEM((past_len, M_KV_RANK), BF16), pltpu.VMEM((nrow, n_blocks), F32),
                            pltpu.VMEM((nrow, M_KV_RANK), BF16), pltpu.VMEM((nrow, LANES), F32)]),
        out_shape=[jax.ShapeDtypeStruct((nseq, nrow, past_len), F32),
                   jax.ShapeDtypeStruct((nseq, nrow, LANES), jnp.int32),
                   jax.ShapeDtypeStruct((nseq, nrow, LANES), F32),
                   jax.ShapeDtypeStruct((nseq, n_q, A_W), F32),
                   jax.ShapeDtypeStruct((nseq, n_q, M_W), F32)],
        compiler_params=_cparams("parallel", "arbitrary"),
        name="paged_scores",
    )(page_table, *([cache_k] * npp), *([cache_ckv] * npp), *([cache_kr] * npp), *seq_in,
      consts["cos_t"], consts["sin_t"], *const_in)

    pages_per_block = A_BLOCK // PAGE_SIZE
    n_slot = nrow * A_TOPK
    blocks = top[:, :, :A_TOPK]
    weight_rows = (blocks + (jnp.arange(nrow, dtype=jnp.int32) * n_blocks)[None, :, None]).reshape(-1)
    pages = (blocks[..., None] * pages_per_block + jnp.arange(pages_per_block, dtype=jnp.int32)).reshape(nseq, -1)
    page_ids = jnp.take_along_axis(page_table, pages, axis=1).reshape(-1)
    weights = pa.reshape(nseq, nrow * n_blocks, 1, A_BLOCK)

    p_specs = [pl.BlockSpec((1, 1, 1, A_BLOCK), lambda b, wr, pg, t=t: (b, wr[b * n_slot + t], 0, 0))
               for t in range(n_slot)]
    v_specs = [pl.BlockSpec((1, A_DH, PAGE_SIZE),
                            lambda b, wr, pg, t=t, g=g: (pg[(b * n_slot + t) * pages_per_block + g],
                                                         (t // A_TOPK) % A_HEADS, 0))
               for t in range(n_slot) for g in range(pages_per_block)]
    oa_sel = pl.pallas_call(
        functools.partial(_selected_values_kernel, nrow=nrow),
        grid_spec=pltpu.PrefetchScalarGridSpec(
            num_scalar_prefetch=2,
            grid=(nseq,),
            in_specs=p_specs + v_specs + [pl.BlockSpec((1, nrow, LANES), lambda b, wr, pg: (b, 0, 0))],
            out_specs=pl.BlockSpec((1, nrow, A_DH), lambda b, wr, pg: (b, 0, 0))),
        out_shape=jax.ShapeDtypeStruct((nseq, nrow, A_DH), F32),
        compiler_params=_cparams("parallel"),
        name="selected_values",
    )(weight_rows, page_ids, *([weights] * len(p_specs)), *([cache_v] * len(v_specs)), den)
    return oa_sel.reshape(nseq, n_q, A_W) + oa_own, om


def _cross_sample_kernel(q_ref, k_ref, v_ref, o_ref):
    nrow = q_ref.shape[1]
    ncol = k_ref.shape[1]
    s = _dot_nt(q_ref[0], k_ref[0].astype(BF16)) * X_SCALE
    row_h = lax.broadcasted_iota(jnp.int32, (nrow, ncol), 0) % X_HEADS
    col_h = lax.broadcasted_iota(jnp.int32, (nrow, ncol), 1) % X_HEADS
    s = jnp.where(row_h == col_h, s, NEG_INF)
    p = jnp.exp(s - jnp.max(s, axis=1, keepdims=True))
    p = p / jnp.sum(p, axis=1, keepdims=True)
    o_ref[0] = _dot(p.astype(BF16), v_ref[0].astype(BF16))


def _cross_sample(xq_rows, mem_k, mem_v):
    nseq, nrow, _ = xq_rows.shape
    ncol = mem_k.shape[1]
    kspec = pl.BlockSpec((1, ncol, X_DH), lambda b: (b, 0, 0))
    qspec = pl.BlockSpec((1, nrow, X_DH), lambda b: (b, 0, 0))
    return pl.pallas_call(
        _cross_sample_kernel,
        grid=(nseq,),
        in_specs=[qspec, kspec, kspec],
        out_specs=qspec,
        out_shape=jax.ShapeDtypeStruct((nseq, nrow, X_DH), F32),
        compiler_params=_cparams("parallel"),
        name="cross_sample",
    )(xq_rows, mem_k, mem_v)


def _mla_perm():
    nope = [h * M_QK + d for h in range(M_HEADS) for d in range(M_NOPE)]
    lo = [h * M_QK + M_NOPE + j for h in range(M_HEADS) for j in range(M_HALF)]
    hi = [h * M_QK + M_NOPE + M_HALF + j for h in range(M_HEADS) for j in range(M_HALF)]
    return np.array(nope + lo + hi, np.int32)


def _static_consts():
    perm = _mla_perm()
    head7 = perm // M_QK
    dim7 = perm % M_QK
    e64 = (np.arange(A_W)[:, None] // A_DH == np.arange(A_W)[None, :] // A_DH)
    e96 = head7[:, None] == head7[None, :]
    tkr = np.zeros((M_ROPE, 2 * LANES), np.float32)
    for h in range(M_HEADS):
        for j in range(M_HALF):
            tkr[j, h * M_HALF + j] = 1.0
            tkr[M_HALF + j, LANES + h * M_HALF + j] = 1.0
    return perm, head7, dim7, e64.astype(np.float32), e96.astype(np.float32), tkr


def _rope_tables(pos):
    inv = ROPE_BASE ** (-jnp.arange(M_HALF, dtype=F32) / M_HALF)
    ang = pos.astype(F32)[:, None] * inv
    return jnp.cos(ang), jnp.sin(ang)


def _row_consts(n_q, heads, width, head_of_lane):
    r = np.arange(n_q * heads)
    hm = (head_of_lane[None, :] == (r % heads)[:, None]).astype(np.float32)
    rowc = np.zeros((n_q * heads, LANES), np.float32)
    rowc[:, 0] = 2.0 ** (-8.0 * ((r % heads) + 1) / heads)
    rowc[:, 1] = r // heads
    assert hm.shape[1] == width
    return hm, rowc


def kernel(x_prompt, mem_prompt, x_sample, cache_moba_k, cache_moba_v, cache_mla_ckv, cache_mla_kr, cache_mem_k, cache_mem_v, page_table, g_ffn1, w1_gate, w1_up, w1_down, g_mix, w_in, g_aq, g_ak, g_mqa, w_uq, g_mq, g_mkv, w_ukv, g_mk, g_mem, w_mem_kv, g_xq, g_xk, w_pa, w_pm, w_px, w_o, g_ffn2, w2_gate, w2_up, w2_down):
    depth = w_in.shape[0]
    assert depth == 1
    b, s, d = x_prompt.shape
    nseq, n_q, _ = x_sample.shape
    n_mem = mem_prompt.shape[1]
    n_pages = page_table.shape[1]
    past_len = n_pages * PAGE_SIZE
    assert past_len % A_BLOCK == 0 and past_len // A_BLOCK >= A_TOPK
    assert n_pages % min(SCORE_PAGES_PER_STEP, n_pages) == 0 and min(SCORE_PAGES_PER_STEP, n_pages) % GROUP_PAGES == 0
    assert s % 512 == 0 and (nseq * n_q) % A_BLOCK == 0
    l = 0

    perm, head7, dim7, e64, e96, tkr = _static_consts()

    def row(g):
        return g.reshape(1, -1).astype(F32)

    w_in_l = w_in[l]
    o = 3 * A_W + M_Q_RANK + M_KV_RANK
    kr_cols = w_in_l[:, o:o + M_ROPE]
    xq_cols = w_in_l[:, o + M_ROPE:o + M_ROPE + X_W]
    gate_cols = w_in_l[:, o + M_ROPE + X_W:]
    n_proj = o + X_W + M_ROPE
    pad = (-n_proj) % LANES
    wp = jnp.concatenate([w_in_l[:, :o], xq_cols, kr_cols, jnp.zeros((d, pad), F32)], axis=1).astype(BF16)
    w_ukv_l = w_ukv[l].reshape(M_KV_RANK, M_HEADS, M_NOPE + M_DV)
    wkn = w_ukv_l[:, :, :M_NOPE].reshape(M_KV_RANK, M_W)
    wv = w_ukv_l[:, :, M_NOPE:].reshape(M_KV_RANK, M_W)
    pw = {
        "wp": wp,
        "e64": jnp.asarray(e64, BF16),
        "e96": jnp.asarray(e96, BF16),
        "wuq": w_uq[l][:, perm].astype(BF16),
        "wukv": jnp.concatenate([wkn, wv], axis=1).astype(BF16),
        "tkr": jnp.asarray(tkr, BF16),
        "g_aq": row(jnp.tile(g_aq[l], A_HEADS)),
        "g_ak": row(jnp.tile(g_ak[l], A_HEADS)),
        "g_mqa": row(g_mqa[l]),
        "g_mq": row(g_mq[l][dim7]),
        "g_mkv": row(g_mkv[l]),
        "g_mk": row(g_mk[l][dim7]),
        "g_xq": row(g_xq[l]),
    }
    mw = {"w_gates": gate_cols.astype(BF16), "w_pa": w_pa[l].astype(BF16), "w_pm": w_pm[l].astype(BF16),
          "w_px": w_px[l].astype(BF16), "w_o": w_o[l].astype(BF16)}
    ffn1 = (row(g_ffn1[l]), w1_gate[l].astype(BF16), w1_up[l].astype(BF16), w1_down[l].astype(BF16), row(g_mix[l]))
    ffn2 = (row(g_ffn2[l]), w2_gate[l].astype(BF16), w2_up[l].astype(BF16), w2_down[l].astype(BF16), row(g_mix[l]))
    slopes = 2.0 ** (-8.0 * jnp.arange(1, A_HEADS + 1, dtype=F32) / A_HEADS)

    def lane_tables(pos):
        c, sn = _rope_tables(pos)
        return jnp.tile(c, (1, M_HEADS)), jnp.tile(sn, (1, M_HEADS))

    t = b * s
    hp, up = _ffn(x_prompt.reshape(t, d), *ffn1, with_u=True)
    cos_p, sin_p = lane_tables(jnp.arange(s))
    pr = _project(up, cos_p, sin_p, pw, tm=512, batch=b)
    mk, mv = _memory_kv(mem_prompt.reshape(b * n_mem, d), row(g_mem[l]), w_mem_kv[l].astype(BF16), row(g_xk[l]))

    def bsd(a):
        return a.reshape(b, s, a.shape[-1])

    oa = _moba_prompt(bsd(pr["aq"]), bsd(pr["ak_b"]), pr["av_b"],
                      pr["kmean"].reshape(b, s // A_BLOCK, A_W).astype(BF16), slopes)
    om = _mla_prompt(bsd(pr["mq"]), bsd(pr["km"]), pr["vm"])
    ox = _cross_prompt(bsd(pr["xq"]), mk.reshape(b, n_mem, X_W), mv.reshape(b, n_mem, X_W))
    h2 = _merge(up, oa.reshape(t, A_W), om.reshape(t, M_W), ox.reshape(t, X_W), hp, mw)
    yp = _ffn(h2, *ffn2, with_u=False).reshape(b, s, d)

    ts = nseq * n_q
    hs, us = _ffn(x_sample.reshape(ts, d), *ffn1, with_u=True)
    pos_s = past_len + (jnp.arange(ts) % n_q)
    cos_s, sin_s = lane_tables(pos_s)
    ps = _project(us, cos_s, sin_s, pw, tm=min(ts, 512))

    nrow = n_q * A_HEADS
    hm5, rowc = _row_consts(n_q, A_HEADS, A_W, np.arange(A_W) // A_DH)
    hm7, _ = _row_consts(n_q, M_HEADS, M_QW, np.where(np.arange(M_QW) < M_W, np.arange(M_QW) // M_NOPE,
                                                        (np.arange(M_QW) % LANES) // M_HALF))

    def rep_rows(a, heads):
        return jnp.repeat(a.reshape(nseq, n_q, a.shape[-1]), heads, axis=1)

    def pad_new(a):
        a = a.reshape(nseq, n_q, a.shape[-1])
        return jnp.pad(a, ((0, 0), (0, LANES - n_q), (0, 0)))

    mq_s = ps["mq"].astype(F32)
    q_lo = mq_s[:, A_W:A_W + LANES].reshape(nseq, n_q, M_HEADS, M_HALF)
    q_hi = mq_s[:, A_W + LANES:].reshape(nseq, n_q, M_HEADS, M_HALF)
    g_r = g_mk[l][M_NOPE:].astype(F32)
    ar_cos = jnp.concatenate([q_lo, q_hi], axis=-1) * g_r
    ar_sin = jnp.concatenate([q_hi, -q_lo], axis=-1) * g_r
    ar = (jnp.concatenate([ar_cos, ar_sin], axis=-1) * M_SCALE).reshape(nseq, nrow, 2 * M_ROPE)

    c_all, s_all = _rope_tables(jnp.arange(past_len))
    n_blocks = past_len // A_BLOCK
    expand = (np.arange(past_len)[None, :] // A_BLOCK == np.arange(n_blocks)[:, None]).astype(np.float32)
    sc = {
        "wkn_t": wkn.T.astype(BF16),
        "wv": wv.astype(BF16),
        "g_nope": row(jnp.tile(g_mk[l][:M_NOPE], M_HEADS)),
        "hm5": jnp.asarray(hm5), "hm7": jnp.asarray(hm7), "rowc": jnp.asarray(rowc),
        "expand": jnp.asarray(expand, BF16),
        "cos_t": jnp.tile(c_all.T, (2, 1)), "sin_t": jnp.tile(s_all.T, (2, 1)),
    }
    sq = {
        "qa_rep": rep_rows(ps["aq"], A_HEADS),
        "qn_rep": rep_rows(ps["mq"][:, :M_W], M_HEADS),
        "qm_rep": rep_rows(ps["mq"], M_HEADS),
        "ar": ar,
        "ka_new": pad_new(ps["ak_b"]), "va_new": pad_new(ps["av_b"]),
        "km_new": pad_new(ps["km"]), "vm_new": pad_new(ps["vm"]),
    }
    n_pool = cache_moba_k.shape[1]

    def keys_on_lanes(c):
        c = c[l]
        return jnp.moveaxis(c, 1, -1).reshape(n_pool, -1, PAGE_SIZE)

    oa2, om2 = _sample_mixers(page_table, keys_on_lanes(cache_moba_k), keys_on_lanes(cache_moba_v),
                              cache_mla_ckv[l], keys_on_lanes(cache_mla_kr), sq, sc)
    ox2 = _cross_sample(ps["xq"].reshape(nseq, n_q * X_HEADS, X_DH),
                        cache_mem_k[l].reshape(nseq, n_mem * X_HEADS, X_DH),
                        cache_mem_v[l].reshape(nseq, n_mem * X_HEADS, X_DH))
    hs2 = _merge(us, oa2.reshape(ts, A_W), om2.reshape(ts, M_W), ox2.reshape(ts, X_W), hs, mw)
    ys = _ffn(hs2, *ffn2, with_u=False).reshape(nseq, n_q, d)

    def stack(a, shape):
        return a.reshape((1,) + shape)

    def tokens_back(a, shape):
        return jnp.moveaxis(a.reshape((b,) + shape + (s,)), -1, 1)[None]

    return (yp, ys,
            tokens_back(pr["ak"], (A_HEADS, A_DH)), tokens_back(pr["av"], (A_HEADS, A_DH)),
            stack(pr["ckv"], (b, s, M_KV_RANK)), tokens_back(pr["kr"], (M_ROPE,)),
            stack(mk, (b, n_mem, X_HEADS, X_DH)), stack(mv, (b, n_mem, X_HEADS, X_DH)),
            stack(ps["ak"], (nseq, n_q, A_HEADS, A_DH)), stack(ps["av"], (nseq, n_q, A_HEADS, A_DH)),
            stack(ps["ckv"], (nseq, n_q, M_KV_RANK)), stack(ps["kr"], (nseq, n_q, M_ROPE)))
```

```python
import functools

import numpy as np
import jax
import jax.numpy as jnp
from jax import lax
from jax.experimental import pallas as pl
from jax.experimental.pallas import tpu as pltpu

F32 = jnp.float32
BF16 = jnp.bfloat16

EPS = 1e-6
ROPE_BASE = 10000.0
PAGE_SIZE = 128
A_HEADS, A_DH, A_BLOCK, A_TOPK = 8, 64, 256, 3
A_W = A_HEADS * A_DH
A_SCALE = A_DH ** -0.5
M_HEADS, M_Q_RANK, M_KV_RANK, M_NOPE, M_ROPE, M_DV = 8, 256, 128, 64, 32, 64
M_QK = M_NOPE + M_ROPE
M_HALF = M_ROPE // 2
M_W = M_HEADS * M_DV
M_QW = M_HEADS * M_QK
M_SCALE = M_QK ** -0.5
X_HEADS, X_DH = 4, 128
X_W = X_HEADS * X_DH
X_SCALE = X_DH ** -0.5
N_BRANCH = 3
LANES = 128
VMEM_LIMIT = 56 * 1024 * 1024
NEG_INF = float("-inf")
SCORE_PAGES_PER_STEP = 32
GROUP_PAGES = 8
LANE_CHUNK = 2048

_NT = (((1,), (1,)), ((), ()))


def _dot(a, b):
    return jnp.dot(a, b, preferred_element_type=F32)


def _dot_nt(a, b):
    return lax.dot_general(a, b, _NT, preferred_element_type=F32)


def _cparams(*sem):
    return pltpu.CompilerParams(dimension_semantics=sem, vmem_limit_bytes=VMEM_LIMIT)


def _rms(x, g):
    return x * lax.rsqrt(jnp.mean(x * x, axis=-1, keepdims=True) + EPS) * g


def _split_hi_lo(x):
    hi = x.astype(BF16)
    lo = (x - hi.astype(F32)).astype(BF16)
    return hi, lo


def _top3_mask(g, idx, n, axis=1):
    sel, _ = _top3(g, idx, n, axis)
    return sel


def _top3(g, idx, n, axis=1):
    sel = jnp.zeros_like(g)
    chosen = []
    for _ in range(A_TOPK):
        m = jnp.max(g, axis=axis, keepdims=True)
        first = jnp.min(jnp.where(g == m, idx, float(n)), axis=axis, keepdims=True)
        pick = idx == first
        sel = jnp.where(jnp.logical_and(pick, m > NEG_INF), 1.0, sel)
        g = jnp.where(pick, NEG_INF, g)
        chosen.append(first)
    return sel, chosen


def _ffn_kernel(x_ref, g_ref, wg_ref, wu_ref, wd_ref, g2_ref, *rest, with_u):
    if with_u:
        y_ref, u_ref, xn_sc, acc_sc = rest
    else:
        y_ref, xn_sc, acc_sc = rest
    k = pl.program_id(1)

    @pl.when(k == 0)
    def _():
        xn_sc[...] = _rms(x_ref[...], g_ref[...]).astype(BF16)
        acc_sc[...] = jnp.zeros_like(acc_sc)

    xn = xn_sc[...]
    gate = _dot(xn, wg_ref[...])
    up = _dot(xn, wu_ref[...])
    hid = gate * jax.nn.sigmoid(gate) * up
    acc_sc[...] += _dot(hid.astype(BF16), wd_ref[...])

    @pl.when(k == pl.num_programs(1) - 1)
    def _():
        y = x_ref[...] + 0.5 * acc_sc[...]
        y_ref[...] = y
        if with_u:
            u_ref[...] = _rms(y, g2_ref[...]).astype(BF16)


def _ffn(x, g, wg, wu, wd, g2, with_u):
    t, d = x.shape
    dff = wg.shape[1]
    tm = min(t, 512)
    tf = dff // 2 if (dff // 2) % LANES == 0 else dff
    grid = (t // tm, dff // tf)
    tok = pl.BlockSpec((tm, d), lambda i, k: (i, 0))
    vec = pl.BlockSpec((1, d), lambda i, k: (0, 0))
    out_shape = [jax.ShapeDtypeStruct((t, d), F32)]
    out_specs = [tok]
    if with_u:
        out_shape.append(jax.ShapeDtypeStruct((t, d), BF16))
        out_specs.append(tok)
    res = pl.pallas_call(
        functools.partial(_ffn_kernel, with_u=with_u),
        grid=grid,
        in_specs=[tok, vec,
                  pl.BlockSpec((d, tf), lambda i, k: (0, k)),
                  pl.BlockSpec((d, tf), lambda i, k: (0, k)),
                  pl.BlockSpec((tf, d), lambda i, k: (k, 0)),
                  vec],
        out_specs=out_specs,
        out_shape=out_shape,
        scratch_shapes=[pltpu.VMEM((tm, d), BF16), pltpu.VMEM((tm, d), F32)],
        compiler_params=_cparams("parallel", "arbitrary"),
        name="ffn",
    )(x, g, wg, wu, wd, g2)
    return res if with_u else res[0]


def _project_kernel(u_ref, cos_ref, sin_ref, wp_ref, e64_ref, e96_ref, wuq_ref, wukv_ref, tkr_ref,
                    gaq_ref, gak_ref, gmqa_ref, gmq_ref, gmkv_ref, gmk_ref, gxq_ref,
                    aq_ref, akf_ref, akb_ref, avf_ref, avb_ref, mq_ref, ckv_ref, kr_ref, xq_ref,
                    km_ref, vm_ref, kmean_ref, *, tokens_on_lanes):
    tm = u_ref.shape[0]

    def store_rows(x, *out_refs):
        if tokens_on_lanes:
            x = x.T
        for ref in out_refs:
            if tokens_on_lanes:
                ref[0] = x[:ref.shape[1]].astype(ref.dtype)
            else:
                ref[...] = x[:, :ref.shape[1]].astype(ref.dtype)

    z = _dot(u_ref[...], wp_ref[...])
    cs = cos_ref[...]
    sn = sin_ref[...]

    def group_norm(x, e_ref, n):
        ss = _dot((x * x).astype(BF16), e_ref[...])
        return x * lax.rsqrt(ss * (1.0 / n) + EPS)

    def rope_store(xn, out_ref):
        x1 = xn[:, A_W:A_W + LANES]
        x2 = xn[:, A_W + LANES:]
        out_ref[:, :A_W] = xn[:, :A_W].astype(BF16)
        out_ref[:, A_W:A_W + LANES] = (x1 * cs - x2 * sn).astype(BF16)
        out_ref[:, A_W + LANES:] = (x1 * sn + x2 * cs).astype(BF16)

    aq = group_norm(z[:, 0:A_W], e64_ref, A_DH) * gaq_ref[...]
    aq_ref[...] = aq.astype(BF16)
    ak = group_norm(z[:, A_W:2 * A_W], e64_ref, A_DH) * gak_ref[...]
    store_rows(ak, akf_ref)
    akb_ref[...] = ak.astype(BF16)
    for blk in range(tm // A_BLOCK):
        kmean_ref[blk] = jnp.mean(ak[blk * A_BLOCK:(blk + 1) * A_BLOCK], axis=0, keepdims=True)
    av = z[:, 2 * A_W:3 * A_W]
    store_rows(av, avf_ref, avb_ref)

    o = 3 * A_W
    mqa = _rms(z[:, o:o + M_Q_RANK], gmqa_ref[...])
    mq = _dot(mqa.astype(BF16), wuq_ref[...])
    rope_store(group_norm(mq, e96_ref, M_QK) * gmq_ref[...], mq_ref)

    o += M_Q_RANK
    ckv = _rms(z[:, o:o + M_KV_RANK], gmkv_ref[...])
    ckv_ref[...] = ckv
    o += M_KV_RANK
    xq = z[:, o:o + X_W]
    o += X_W
    kr = z[:, o:o + M_ROPE]
    store_rows(z[:, o:o + LANES], kr_ref)
    kv = _dot(ckv.astype(BF16), wukv_ref[...])
    store_rows(kv[:, M_W:], vm_ref)
    kr_hi, kr_lo = _split_hi_lo(kr)
    kr_t = _dot(kr_hi, tkr_ref[...]) + _dot(kr_lo, tkr_ref[...])
    kpre = jnp.concatenate([kv[:, :M_W], kr_t], axis=1)
    rope_store(group_norm(kpre, e96_ref, M_QK) * gmk_ref[...], km_ref)

    for h in range(X_HEADS):
        xh = xq[:, h * X_DH:(h + 1) * X_DH]
        xq_ref[:, h * X_DH:(h + 1) * X_DH] = _rms(xh, gxq_ref[...]).astype(BF16)


def _project(u, cos_t, sin_t, w, tm, batch=None):
    t = u.shape[0]
    n_rope_blocks = cos_t.shape[0] // tm
    grid = (t // tm,)
    nb = tm // A_BLOCK
    cache_rows = ("ak", "av", "av_b", "kr", "vm")

    def tok(width):
        return pl.BlockSpec((tm, width), lambda i: (i, 0))

    def out_shape_spec(name, width, dt):
        if batch is None or name not in cache_rows:
            return jax.ShapeDtypeStruct((t, width), dt), tok(width)
        per_seq = t // batch // tm
        return (jax.ShapeDtypeStruct((batch, width, t // batch), dt),
                pl.BlockSpec((1, width, tm), lambda i: (i // per_seq, 0, i % per_seq)))

    def full(a):
        return pl.BlockSpec(a.shape, lambda i: (0,) * a.ndim)

    rope = pl.BlockSpec((tm, LANES), lambda i: (i % n_rope_blocks, 0))
    consts = [w["wp"], w["e64"], w["e96"], w["wuq"], w["wukv"], w["tkr"],
              w["g_aq"], w["g_ak"], w["g_mqa"], w["g_mq"], w["g_mkv"], w["g_mk"], w["g_xq"]]
    outs = [("aq", A_W, BF16), ("ak", A_W, F32), ("ak_b", A_W, BF16), ("av", A_W, F32), ("av_b", A_W, BF16),
            ("mq", M_QW, BF16), ("ckv", M_KV_RANK, F32), ("kr", M_ROPE, F32), ("xq", X_W, BF16),
            ("km", M_QW, BF16), ("vm", M_W, BF16)]
    out_shape, out_specs = (list(x) for x in zip(*(out_shape_spec(*o) for o in outs)))
    out_shape.append(jax.ShapeDtypeStruct((t // A_BLOCK, 1, A_W), F32))
    out_specs.append(pl.BlockSpec((nb, 1, A_W), lambda i: (i, 0, 0)))
    res = pl.pallas_call(
        functools.partial(_project_kernel, tokens_on_lanes=batch is not None),
        grid=grid,
        in_specs=[tok(u.shape[1]), rope, rope] + [full(a) for a in consts],
        out_specs=out_specs,
        out_shape=out_shape,
        compiler_params=_cparams("parallel"),
        name="project",
    )(u, cos_t, sin_t, *consts)
    named = {name: r for (name, _, _), r in zip(outs, res[:-1])}
    named["kmean"] = res[-1]
    return named


def _attend_blocks(n_blocks, tq, score_block, v_t_ref, rows, s_sc, p_sc):
    m = None
    for b in range(n_blocks):
        s_t = score_block(b)
        s_sc[b * tq:(b + 1) * tq, :] = s_t
        bm = jnp.max(s_t, axis=0, keepdims=True)
        m = bm if m is None else jnp.maximum(m, bm)
    l = None
    for b in range(n_blocks):
        p = jnp.exp(s_sc[b * tq:(b + 1) * tq, :] - m)
        p_sc[b * tq:(b + 1) * tq, :] = p.astype(BF16)
        bl = jnp.sum(p, axis=0, keepdims=True)
        l = bl if l is None else l + bl
    nk = n_blocks * tq
    return _dot(v_t_ref[0, rows, 0:nk], p_sc[0:nk, :]) / l


def _per_tile_index(i, n_tiles, body):
    for ii in range(n_tiles):
        @pl.when(i == ii)
        def _(ii=ii):
            body(ii)


def _head_rows(e, dv):
    return slice(e * dv, (e + 1) * dv)


def _moba_kernel(slope_ref, q_ref, k_ref, vt_ref, km_ref, o_ref, s_sc, p_sc):
    tq = q_ref.shape[1]
    pair = pl.program_id(1)
    i = pl.program_id(2)
    nblk = km_ref.shape[1]
    qf = q_ref[0].astype(F32) * A_SCALE
    lane = lax.broadcasted_iota(jnp.int32, (tq, LANES), 1)
    rel_t = (lax.broadcasted_iota(jnp.int32, (tq, tq), 1)
             - lax.broadcasted_iota(jnp.int32, (tq, tq), 0)).astype(F32)
    rowb = lax.broadcasted_iota(jnp.int32, (nblk, tq), 0).astype(F32)
    i_f = i.astype(F32)
    qs, sels, slopes = [], [], []
    for e in range(2):
        in_head = (lane < A_DH) if e == 0 else (lane >= A_DH)
        qe = jnp.where(in_head, qf, 0.0).astype(BF16)
        gate_t = _dot_nt(km_ref[0], qe)
        sels.append(_top3_mask(jnp.where(rowb < i_f, gate_t, NEG_INF), rowb, nblk, axis=0))
        qs.append(qe)
        slopes.append(slope_ref[2 * pair + e])

    def body(ii):
        nk = (ii + 1) * tq
        outs = []
        for e in range(2):
            s_all = _dot_nt(k_ref[0, 0:nk, :], qs[e])
            bias = slopes[e] * rel_t

            def score_block(b, e=e, s_all=s_all, bias=bias):
                s_t = s_all[b * tq:(b + 1) * tq] - bias - slopes[e] * float((ii - b) * tq)
                keep = (rel_t >= 0.0) if b == ii else (sels[e][b:b + 1] > 0.5)
                return jnp.where(keep, s_t, NEG_INF)

            outs.append(_attend_blocks(ii + 1, tq, score_block, vt_ref, _head_rows(e, A_DH),
                                       s_sc.at[e], p_sc.at[e]))
        o_ref[0] = jnp.concatenate(outs, axis=0).T.astype(BF16)

    _per_tile_index(i, nblk, body)


def _attn_scratch(s, tq):
    return [pltpu.VMEM((2, s, tq), F32), pltpu.VMEM((2, s, tq), BF16)]


def _moba_prompt(aq, ak, av_t, kmean, slopes):
    b, s, _ = aq.shape
    tq = A_BLOCK
    nblk = s // A_BLOCK
    qspec = pl.BlockSpec((1, tq, LANES), lambda bi, p, i: (bi, i, p))
    return pl.pallas_call(
        _moba_kernel,
        grid=(b, A_W // LANES, s // tq),
        in_specs=[pl.BlockSpec(memory_space=pltpu.SMEM), qspec,
                  pl.BlockSpec((1, s, LANES), lambda bi, p, i: (bi, 0, p)),
                  pl.BlockSpec((1, LANES, s), lambda bi, p, i: (bi, p, 0)),
                  pl.BlockSpec((1, nblk, LANES), lambda bi, p, i: (bi, 0, p))],
        out_specs=qspec,
        out_shape=jax.ShapeDtypeStruct((b, s, A_W), BF16),
        scratch_shapes=_attn_scratch(s, tq),
        compiler_params=_cparams("parallel", "parallel", "arbitrary"),
        name="moba_prompt",
    )(slopes, aq, ak, av_t, kmean)


def _mla_kernel(qn_ref, qr_ref, kn_ref, kr_ref, vt_ref, o_ref, s_sc, p_sc):
    tq = qn_ref.shape[1]
    pair = pl.program_id(1)
    i = pl.program_id(2)
    n_tiles = kn_ref.shape[1] // tq
    qn = qn_ref[0].astype(F32)
    qr = qr_ref[0].astype(F32)
    lane = lax.broadcasted_iota(jnp.int32, (tq, LANES), 1)
    lane_r = lax.broadcasted_iota(jnp.int32, (tq, 2 * LANES), 1) % LANES
    rel_t = (lax.broadcasted_iota(jnp.int32, (tq, tq), 1)
             - lax.broadcasted_iota(jnp.int32, (tq, tq), 0))
    qs = []
    for e in range(2):
        head = 2 * pair + e
        in_head = (lane < M_NOPE) if e == 0 else (lane >= M_NOPE)
        in_rope = jnp.logical_and(lane_r >= head * M_HALF, lane_r < (head + 1) * M_HALF)
        qs.append([jnp.where(in_head, qn, 0.0).astype(BF16), jnp.where(in_rope, qr, 0.0).astype(BF16)])

    def body(ii):
        nk = (ii + 1) * tq
        outs = []
        for e in range(2):
            s_all = (_dot_nt(kn_ref[0, 0:nk, :], qs[e][0]) + _dot_nt(kr_ref[0, 0:nk, :], qs[e][1])) * M_SCALE

            def score_block(b, s_all=s_all):
                s_t = s_all[b * tq:(b + 1) * tq]
                return jnp.where(rel_t >= 0, s_t, NEG_INF) if b == ii else s_t

            outs.append(_attend_blocks(ii + 1, tq, score_block, vt_ref, _head_rows(e, M_DV),
                                       s_sc.at[e], p_sc.at[e]))
        o_ref[0] = jnp.concatenate(outs, axis=0).T.astype(BF16)

    _per_tile_index(i, n_tiles, body)


def _mla_prompt(mq, km, vm_t):
    b, s, _ = mq.shape
    tq = A_BLOCK
    rope_blk = M_W // (2 * LANES)
    qspec = pl.BlockSpec((1, tq, LANES), lambda bi, p, i: (bi, i, p))
    return pl.pallas_call(
        _mla_kernel,
        grid=(b, M_W // LANES, s // tq),
        in_specs=[qspec, pl.BlockSpec((1, tq, 2 * LANES), lambda bi, p, i: (bi, i, rope_blk)),
                  pl.BlockSpec((1, s, LANES), lambda bi, p, i: (bi, 0, p)),
                  pl.BlockSpec((1, s, 2 * LANES), lambda bi, p, i: (bi, 0, rope_blk)),
                  pl.BlockSpec((1, LANES, s), lambda bi, p, i: (bi, p, 0))],
        out_specs=qspec,
        out_shape=jax.ShapeDtypeStruct((b, s, M_W), BF16),
        scratch_shapes=_attn_scratch(s, tq),
        compiler_params=_cparams("parallel", "parallel", "arbitrary"),
        name="mla_prompt",
    )(mq, mq, km, km, vm_t)


def _memkv_kernel(mem_ref, gmem_ref, w_ref, gxk_ref, mk_ref, mv_ref):
    kv = _dot(_rms(mem_ref[...], gmem_ref[...]).astype(BF16), w_ref[...])
    for h in range(X_HEADS):
        mk_ref[:, h * X_DH:(h + 1) * X_DH] = _rms(kv[:, h * X_DH:(h + 1) * X_DH], gxk_ref[...])
    mv_ref[...] = kv[:, X_W:]


def _memory_kv(mem, g_mem, w, g_xk):
    t, d = mem.shape
    tm = min(t, 512)
    out = jax.ShapeDtypeStruct((t, X_W), F32)
    ospec = pl.BlockSpec((tm, X_W), lambda i: (i, 0))
    return pl.pallas_call(
        _memkv_kernel,
        grid=(t // tm,),
        in_specs=[pl.BlockSpec((tm, d), lambda i: (i, 0)), pl.BlockSpec((1, d), lambda i: (0, 0)),
                  pl.BlockSpec(w.shape, lambda i: (0, 0)), pl.BlockSpec((1, X_DH), lambda i: (0, 0))],
        out_specs=[ospec, ospec],
        out_shape=[out, out],
        compiler_params=_cparams("parallel"),
        name="memory_kv",
    )(mem, g_mem, w, g_xk)


def _cross_kernel(q_ref, k_ref, v_ref, o_ref):
    for h in range(X_HEADS):
        sl = slice(h * X_DH, (h + 1) * X_DH)
        s = _dot_nt(q_ref[0, :, sl], k_ref[0, :, sl].astype(BF16)) * X_SCALE
        p = jnp.exp(s - jnp.max(s, axis=1, keepdims=True))
        p = p / jnp.sum(p, axis=1, keepdims=True)
        o_ref[0, :, sl] = _dot(p.astype(BF16), v_ref[0, :, sl].astype(BF16)).astype(BF16)


def _cross_prompt(xq, mk, mv):
    b, s, _ = xq.shape
    n_mem = mk.shape[1]
    tq = min(s, 512)
    qspec = pl.BlockSpec((1, tq, X_W), lambda bi, i: (bi, i, 0))
    kspec = pl.BlockSpec((1, n_mem, X_W), lambda bi, i: (bi, 0, 0))
    return pl.pallas_call(
        _cross_kernel,
        grid=(b, s // tq),
        in_specs=[qspec, kspec, kspec],
        out_specs=qspec,
        out_shape=jax.ShapeDtypeStruct((b, s, X_W), BF16),
        compiler_params=_cparams("parallel", "parallel"),
        name="cross_prompt",
    )(xq, mk, mv)


def _merge_kernel(u_ref, oa_ref, om_ref, ox_ref, h_ref, wg_ref, wpa_ref, wpm_ref, wpx_ref, wo_ref, y_ref):
    d = h_ref.shape[1]
    u = u_ref[...]
    comb = None
    for br, (o_ref, wp_ref) in enumerate(((oa_ref, wpa_ref), (om_ref, wpm_ref), (ox_ref, wpx_ref))):
        gate = jax.nn.sigmoid(_dot(u, wg_ref[:, br * d:(br + 1) * d]))
        term = gate * _dot(o_ref[...].astype(BF16), wp_ref[...])
        comb = term if comb is None else comb + term
    y_ref[...] = h_ref[...] + _dot(comb.astype(BF16), wo_ref[...])


def _merge(u, oa, om, ox, h, w):
    t, d = h.shape
    tm = min(t, 512)

    def tok(a):
        return pl.BlockSpec((tm, a.shape[1]), lambda i: (i, 0))

    def full(a):
        return pl.BlockSpec(a.shape, lambda i: (0, 0))

    consts = [w["w_gates"], w["w_pa"], w["w_pm"], w["w_px"], w["w_o"]]
    return pl.pallas_call(
        _merge_kernel,
        grid=(t // tm,),
        in_specs=[tok(u), tok(oa), tok(om), tok(ox), tok(h)] + [full(a) for a in consts],
        out_specs=tok(h),
        out_shape=jax.ShapeDtypeStruct((t, d), F32),
        compiler_params=_cparams("parallel"),
        name="merge",
    )(u, oa, om, ox, h, *consts)


def _fold_heads(res, head_mask, row_q, n_q):
    masked = res * head_mask
    return [jnp.sum(jnp.where(row_q == float(q), masked, 0.0), axis=0, keepdims=True) for q in range(n_q)]


def _paged_scores_kernel(pt_ref, *refs, npp, past_len):
    del pt_ref
    kp_refs = refs[:npp]
    ck_refs = refs[npp:2 * npp]
    kr_refs = refs[2 * npp:3 * npp]
    (qa_ref, qn_ref, qm_ref, ar_ref, ka_ref, va_ref, km_ref, vm_ref, cos_ref, sin_ref, wkn_ref, wv_ref, gn_ref,
     hm5_ref, hm7_ref, rowc_ref, exp_ref,
     pa_ref, top_ref, den_ref, oa_own_ref, om_ref,
     sa_sc, sm_sc, ckv_sc, g_sc, aq_sc, mx_sc) = refs[3 * npp:]
    c = pl.program_id(1)
    nrow = sa_sc.shape[0]
    n_q = nrow // A_HEADS
    n_blocks = g_sc.shape[1]
    colg = lax.broadcasted_iota(jnp.int32, (nrow, n_blocks), 1)

    @pl.when(c == 0)
    def _():
        g_sc[...] = jnp.zeros_like(g_sc)
        mx_sc[...] = jnp.full_like(mx_sc, NEG_INF)
        qn = qn_ref[0].astype(F32) * gn_ref[...] * hm5_ref[...]
        aq_sc[...] = (_dot(qn.astype(BF16), wkn_ref[...]) * M_SCALE).astype(BF16)

    qa = (qa_ref[0].astype(F32) * hm5_ref[...] * A_SCALE).astype(BF16)
    aq = aq_sc[...]
    ar = ar_ref[0].astype(BF16)
    gates = g_sc[...]
    m_run = mx_sc[:, 0:1]
    gkeys = GROUP_PAGES * PAGE_SIZE
    pages_per_block = A_BLOCK // PAGE_SIZE
    for gi in range(npp // GROUP_PAGES):
        pages = range(gi * GROUP_PAGES, (gi + 1) * GROUP_PAGES)
        first_page = c * npp + gi * GROUP_PAGES
        off = pl.multiple_of(first_page * PAGE_SIZE, gkeys)
        s_a = jnp.concatenate([_dot(qa, kp_refs[pi][0].astype(BF16)) for pi in pages], axis=1)
        sa_sc[:, pl.ds(off, gkeys)] = s_a
        for bi in range(GROUP_PAGES // pages_per_block):
            blk = first_page // pages_per_block + bi
            bsum = jnp.sum(s_a[:, bi * A_BLOCK:(bi + 1) * A_BLOCK], axis=1, keepdims=True)
            gates = gates + jnp.where(colg == blk, bsum, 0.0)
        ckb = jnp.concatenate([ck_refs[pi][0].astype(BF16) for pi in pages], axis=0)
        ckv_sc[pl.ds(off, gkeys), :] = ckb
        kn_t = _dot_nt(wkn_ref[...], ckb)
        ssq = jnp.sum((kn_t * kn_t).reshape(M_HEADS, M_NOPE, gkeys), axis=1)
        kr_t = jnp.concatenate([kr_refs[pi][0] for pi in pages], axis=1)
        ssr = jnp.sum(kr_t * kr_t, axis=0, keepdims=True)
        rinv = lax.rsqrt((ssq + ssr) * (1.0 / M_QK) + EPS)
        lsl = slice(gi * gkeys, (gi + 1) * gkeys)
        feat = jnp.concatenate([kr_t * cos_ref[:, lsl], kr_t * sin_ref[:, lsl]], axis=0).astype(BF16)
        s_m = (_dot_nt(aq, ckb) + _dot(ar, feat)) * jnp.concatenate([rinv] * n_q, axis=0)
        sm_sc[:, pl.ds(off, gkeys)] = s_m
        m_run = jnp.maximum(m_run, jnp.max(s_m, axis=1, keepdims=True))
    g_sc[...] = gates
    mx_sc[...] = jnp.broadcast_to(m_run, mx_sc.shape)

    @pl.when(c == pl.num_programs(1) - 1)
    def _():
        slope = rowc_ref[:, 0:1]
        row_q = rowc_ref[:, 1:2]
        tpos = row_q + float(past_len)
        lc = min(LANE_CHUNK, past_len)
        n_ch = past_len // lc
        lane_i = lax.broadcasted_iota(jnp.int32, (nrow, LANES), 1)
        lane_new = lane_i.astype(F32)
        new_ok = jnp.logical_and(lane_new <= row_q, lane_new < float(n_q))

        sel, chosen = _top3(gates, colg.astype(F32), n_blocks)
        sel = sel.astype(BF16)
        top = jnp.zeros((nrow, LANES), F32)
        for k, idx in enumerate(chosen):
            top = jnp.where(lane_i == k, idx, top)
        top_ref[0] = top.astype(jnp.int32)
        s_own = _dot_nt(qa, ka_ref[0]) - slope * (row_q - lane_new)
        s_own = jnp.where(new_ok, s_own, NEG_INF)
        mx = jnp.max(s_own, axis=1, keepdims=True)
        for ch in range(n_ch):
            sl = slice(ch * lc, (ch + 1) * lc)
            kpos = (lax.broadcasted_iota(jnp.int32, (nrow, lc), 1) + ch * lc).astype(F32)
            s = sa_sc[:, sl] - slope * (tpos - kpos)
            s = jnp.where(_dot(sel, exp_ref[:, sl]) > 0.5, s, NEG_INF)
            sa_sc[:, sl] = s
            mx = jnp.maximum(mx, jnp.max(s, axis=1, keepdims=True))
        p_own = jnp.exp(s_own - mx)
        den = jnp.sum(p_own, axis=1, keepdims=True)
        for ch in range(n_ch):
            sl = slice(ch * lc, (ch + 1) * lc)
            p = jnp.exp(sa_sc[:, sl] - mx)
            pa_ref[0, :, sl] = p
            den = den + jnp.sum(p, axis=1, keepdims=True)
        den_ref[0] = jnp.broadcast_to(den, (nrow, LANES))
        rows = _fold_heads(_dot(p_own.astype(BF16), va_ref[0]) / den, hm5_ref[...], row_q, n_q)
        for q in range(n_q):
            oa_own_ref[0, q:q + 1, :] = rows[q]

        qm = (qm_ref[0].astype(F32) * hm7_ref[...]).astype(BF16)
        s_new = jnp.where(new_ok, _dot_nt(qm, km_ref[0]) * M_SCALE, NEG_INF)
        mx = jnp.maximum(m_run, jnp.max(s_new, axis=1, keepdims=True))
        p_new = jnp.exp(s_new - mx)
        den = jnp.sum(p_new, axis=1, keepdims=True)
        pc = jnp.zeros((nrow, M_KV_RANK), F32)
        for ch in range(n_ch):
            sl = slice(ch * lc, (ch + 1) * lc)
            p = jnp.exp(sm_sc[:, sl] - mx)
            den = den + jnp.sum(p, axis=1, keepdims=True)
            pc = pc + _dot(p.astype(BF16), ckv_sc[sl, :])
        pc_hi, pc_lo = _split_hi_lo(pc)
        res = (_dot(pc_hi, wv_ref[...]) + _dot(pc_lo, wv_ref[...]) + _dot(p_new.astype(BF16), vm_ref[0])) / den
        rows = _fold_heads(res, hm5_ref[...], row_q, n_q)
        for q in range(n_q):
            om_ref[0, q:q + 1, :] = rows[q]


def _selected_values_kernel(pt_ref, top_ref, *refs, nrow):
    del pt_ref, top_ref
    n_slot = nrow * A_TOPK
    pages_per_block = A_BLOCK // PAGE_SIZE
    p_refs = refs[:n_slot]
    v_refs = refs[n_slot:n_slot * (1 + pages_per_block)]
    den_ref, o_ref = refs[n_slot * (1 + pages_per_block):]
    for r in range(nrow):
        slots = range(r * A_TOPK, (r + 1) * A_TOPK)
        h = r % A_HEADS
        p = jnp.concatenate([p_refs[t][0, h:h + 1, :] for t in slots], axis=1)
        v_t = jnp.concatenate([v_refs[t * pages_per_block + g][0] for t in slots for g in range(pages_per_block)],
                              axis=1)
        res = _dot_nt(jnp.broadcast_to(p, (8, p.shape[1])).astype(BF16), v_t.astype(BF16))
        o_ref[0, r:r + 1, :] = res[0:1] / den_ref[0, r:r + 1, 0:A_DH]


def _sample_mixers(page_table, cache_k, cache_v, cache_ckv, cache_kr, s, consts):
    nseq, n_pages = page_table.shape
    past_len = n_pages * PAGE_SIZE
    nrow = s["qa_rep"].shape[1]
    n_q = nrow // A_HEADS
    n_blocks = past_len // A_BLOCK

    def pages_per_step(want):
        return max(GROUP_PAGES, min(want, n_pages))

    def page_specs(rows, cols, npp):
        return [pl.BlockSpec((1, rows, cols), lambda b, c, pt, pi=pi: (pt[b, c * npp + pi], 0, 0))
                for pi in range(npp)]

    def seq_spec(a):
        return pl.BlockSpec((1,) + a.shape[1:], lambda b, c, pt: (b, 0, 0))

    def full(a):
        return pl.BlockSpec(a.shape, lambda b, c, pt: (0,) * a.ndim)

    npp = pages_per_step(SCORE_PAGES_PER_STEP)
    table = pl.BlockSpec((M_ROPE, npp * PAGE_SIZE), lambda b, c, pt: (0, c))
    seq_in = [s["qa_rep"], s["qn_rep"], s["qm_rep"], s["ar"], s["ka_new"], s["va_new"], s["km_new"], s["vm_new"]]
    const_in = [consts["wkn_t"], consts["wv"], consts["g_nope"], consts["hm5"], consts["hm7"], consts["rowc"],
                consts["expand"]]
    row_block = pl.BlockSpec((1, nrow, LANES), lambda b, c, pt: (b, 0, 0))
    pa, top, den, oa_own, om = pl.pallas_call(
        functools.partial(_paged_scores_kernel, npp=npp, past_len=past_len),
        grid_spec=pltpu.PrefetchScalarGridSpec(
            num_scalar_prefetch=1,
            grid=(nseq, n_pages // npp),
            in_specs=(page_specs(A_W, PAGE_SIZE, npp) + page_specs(PAGE_SIZE, M_KV_RANK, npp)
                      + page_specs(M_ROPE, PAGE_SIZE, npp)
                      + [seq_spec(a) for a in seq_in] + [table, table] + [full(a) for a in const_in]),
            out_specs=[pl.BlockSpec((1, nrow, past_len), lambda b, c, pt: (b, 0, 0)),
                       row_block, row_block,
                       pl.BlockSpec((1, n_q, A_W), lambda b, c, pt: (b, 0, 0)),
                       pl.BlockSpec((1, n_q, M_W), lambda b, c, pt: (b, 0, 0))],
            scratch_shapes=[pltpu.VMEM((nrow, past_len), F32), pltpu.VMEM((nrow, past_len), F32),
                            pltpu.VMEM((past_len, M_KV_RANK), BF16), pltpu.VMEM((nrow, n_blocks), F32),
                            pltpu.VMEM((nrow, M_KV_RANK), BF16), pltpu.VMEM((nrow, LANES), F32)]),
        out_shape=[jax.ShapeDtypeStruct((nseq, nrow, past_len), F32),
                   jax.ShapeDtypeStruct((nseq, nrow, LANES), jnp.int32),
                   jax.ShapeDtypeStruct((nseq, nrow, LANES), F32),
                   jax.ShapeDtypeStruct((nseq, n_q, A_W), F32),
                   jax.ShapeDtypeStruct((nseq, n_q, M_W), F32)],
        compiler_params=_cparams("parallel", "arbitrary"),
        name="paged_scores",
    )(page_table, *([cache_k] * npp), *([cache_ckv] * npp), *([cache_kr] * npp), *seq_in,
      consts["cos_t"], consts["sin_t"], *const_in)

    pages_per_block = A_BLOCK // PAGE_SIZE
    n_slot = nrow * A_TOPK
    blocks = top[:, :, :A_TOPK]
    block_ids = blocks.reshape(-1)
    pages = (blocks[..., None] * pages_per_block + jnp.arange(pages_per_block, dtype=jnp.int32)).reshape(nseq, -1)
    page_ids = jnp.take_along_axis(page_table, pages, axis=1).reshape(-1)

    p_specs = [pl.BlockSpec((1, A_HEADS, A_BLOCK),
                            lambda b, bk, pg, t=t: (b, t // A_TOPK // A_HEADS, bk[b * n_slot + t]))
               for t in range(n_slot)]
    v_specs = [pl.BlockSpec((1, A_DH, PAGE_SIZE),
                            lambda b, wr, pg, t=t, g=g: (pg[(b * n_slot + t) * pages_per_block + g],
                                                         (t // A_TOPK) % A_HEADS, 0))
               for t in range(n_slot) for g in range(pages_per_block)]
    oa_sel = pl.pallas_call(
        functools.partial(_selected_values_kernel, nrow=nrow),
        grid_spec=pltpu.PrefetchScalarGridSpec(
            num_scalar_prefetch=2,
            grid=(nseq,),
            in_specs=p_specs + v_specs + [pl.BlockSpec((1, nrow, LANES), lambda b, wr, pg: (b, 0, 0))],
            out_specs=pl.BlockSpec((1, nrow, A_DH), lambda b, wr, pg: (b, 0, 0))),
        out_shape=jax.ShapeDtypeStruct((nseq, nrow, A_DH), F32),
        compiler_params=_cparams("parallel"),
        name="selected_values",
    )(block_ids, page_ids, *([pa] * len(p_specs)), *([cache_v] * len(v_specs)), den)
    return oa_sel.reshape(nseq, n_q, A_W) + oa_own, om


def _cross_sample_kernel(q_ref, k_ref, v_ref, o_ref):
    nrow = q_ref.shape[1]
    ncol = k_ref.shape[1]
    s = _dot_nt(q_ref[0], k_ref[0].astype(BF16)) * X_SCALE
    row_h = lax.broadcasted_iota(jnp.int32, (nrow, ncol), 0) % X_HEADS
    col_h = lax.broadcasted_iota(jnp.int32, (nrow, ncol), 1) % X_HEADS
    s = jnp.where(row_h == col_h, s, NEG_INF)
    p = jnp.exp(s - jnp.max(s, axis=1, keepdims=True))
    p = p / jnp.sum(p, axis=1, keepdims=True)
    o_ref[0] = _dot(p.astype(BF16), v_ref[0].astype(BF16))


def _cross_sample(xq_rows, mem_k, mem_v):
    nseq, nrow, _ = xq_rows.shape
    ncol = mem_k.shape[1]
    kspec = pl.BlockSpec((1, ncol, X_DH), lambda b: (b, 0, 0))
    qspec = pl.BlockSpec((1, nrow, X_DH), lambda b: (b, 0, 0))
    return pl.pallas_call(
        _cross_sample_kernel,
        grid=(nseq,),
        in_specs=[qspec, kspec, kspec],
        out_specs=qspec,
        out_shape=jax.ShapeDtypeStruct((nseq, nrow, X_DH), F32),
        compiler_params=_cparams("parallel"),
        name="cross_sample",
    )(xq_rows, mem_k, mem_v)


def _mla_perm():
    nope = [h * M_QK + d for h in range(M_HEADS) for d in range(M_NOPE)]
    lo = [h * M_QK + M_NOPE + j for h in range(M_HEADS) for j in range(M_HALF)]
    hi = [h * M_QK + M_NOPE + M_HALF + j for h in range(M_HEADS) for j in range(M_HALF)]
    return np.array(nope + lo + hi, np.int32)


def _static_consts():
    perm = _mla_perm()
    head7 = perm // M_QK
    dim7 = perm % M_QK
    e64 = (np.arange(A_W)[:, None] // A_DH == np.arange(A_W)[None, :] // A_DH)
    e96 = head7[:, None] == head7[None, :]
    tkr = np.zeros((M_ROPE, 2 * LANES), np.float32)
    for h in range(M_HEADS):
        for j in range(M_HALF):
            tkr[j, h * M_HALF + j] = 1.0
            tkr[M_HALF + j, LANES + h * M_HALF + j] = 1.0
    return perm, head7, dim7, e64.astype(np.float32), e96.astype(np.float32), tkr


def _rope_tables(pos):
    inv = ROPE_BASE ** (-jnp.arange(M_HALF, dtype=F32) / M_HALF)
    ang = pos.astype(F32)[:, None] * inv
    return jnp.cos(ang), jnp.sin(ang)


def _row_consts(n_q, heads, width, head_of_lane):
    r = np.arange(n_q * heads)
    hm = (head_of_lane[None, :] == (r % heads)[:, None]).astype(np.float32)
    rowc = np.zeros((n_q * heads, LANES), np.float32)
    rowc[:, 0] = 2.0 ** (-8.0 * ((r % heads) + 1) / heads)
    rowc[:, 1] = r // heads
    assert hm.shape[1] == width
    return hm, rowc


def kernel(x_prompt, mem_prompt, x_sample, cache_moba_k, cache_moba_v, cache_mla_ckv, cache_mla_kr, cache_mem_k, cache_mem_v, page_table, g_ffn1, w1_gate, w1_up, w1_down, g_mix, w_in, g_aq, g_ak, g_mqa, w_uq, g_mq, g_mkv, w_ukv, g_mk, g_mem, w_mem_kv, g_xq, g_xk, w_pa, w_pm, w_px, w_o, g_ffn2, w2_gate, w2_up, w2_down):
    depth = w_in.shape[0]
    assert depth == 1
    b, s, d = x_prompt.shape
    nseq, n_q, _ = x_sample.shape
    n_mem = mem_prompt.shape[1]
    n_pages = page_table.shape[1]
    past_len = n_pages * PAGE_SIZE
    assert past_len % A_BLOCK == 0 and past_len // A_BLOCK >= A_TOPK
    assert n_pages % min(SCORE_PAGES_PER_STEP, n_pages) == 0 and min(SCORE_PAGES_PER_STEP, n_pages) % GROUP_PAGES == 0
    assert s % 512 == 0 and (nseq * n_q) % A_BLOCK == 0
    l = 0

    perm, head7, dim7, e64, e96, tkr = _static_consts()

    def row(g):
        return g.reshape(1, -1).astype(F32)

    w_in_l = w_in[l]
    o = 3 * A_W + M_Q_RANK + M_KV_RANK
    kr_cols = w_in_l[:, o:o + M_ROPE]
    xq_cols = w_in_l[:, o + M_ROPE:o + M_ROPE + X_W]
    gate_cols = w_in_l[:, o + M_ROPE + X_W:]
    n_proj = o + X_W + M_ROPE
    pad = (-n_proj) % LANES
    wp = jnp.concatenate([w_in_l[:, :o], xq_cols, kr_cols, jnp.zeros((d, pad), F32)], axis=1).astype(BF16)
    w_ukv_l = w_ukv[l].reshape(M_KV_RANK, M_HEADS, M_NOPE + M_DV)
    wkn = w_ukv_l[:, :, :M_NOPE].reshape(M_KV_RANK, M_W)
    wv = w_ukv_l[:, :, M_NOPE:].reshape(M_KV_RANK, M_W)
    pw = {
        "wp": wp,
        "e64": jnp.asarray(e64, BF16),
        "e96": jnp.asarray(e96, BF16),
        "wuq": w_uq[l][:, perm].astype(BF16),
        "wukv": jnp.concatenate([wkn, wv], axis=1).astype(BF16),
        "tkr": jnp.asarray(tkr, BF16),
        "g_aq": row(jnp.tile(g_aq[l], A_HEADS)),
        "g_ak": row(jnp.tile(g_ak[l], A_HEADS)),
        "g_mqa": row(g_mqa[l]),
        "g_mq": row(g_mq[l][dim7]),
        "g_mkv": row(g_mkv[l]),
        "g_mk": row(g_mk[l][dim7]),
        "g_xq": row(g_xq[l]),
    }
    mw = {"w_gates": gate_cols.astype(BF16), "w_pa": w_pa[l].astype(BF16), "w_pm": w_pm[l].astype(BF16),
          "w_px": w_px[l].astype(BF16), "w_o": w_o[l].astype(BF16)}
    ffn1 = (row(g_ffn1[l]), w1_gate[l].astype(BF16), w1_up[l].astype(BF16), w1_down[l].astype(BF16), row(g_mix[l]))
    ffn2 = (row(g_ffn2[l]), w2_gate[l].astype(BF16), w2_up[l].astype(BF16), w2_down[l].astype(BF16), row(g_mix[l]))
    slopes = 2.0 ** (-8.0 * jnp.arange(1, A_HEADS + 1, dtype=F32) / A_HEADS)

    def lane_tables(pos):
        c, sn = _rope_tables(pos)
        return jnp.tile(c, (1, M_HEADS)), jnp.tile(sn, (1, M_HEADS))

    t = b * s
    hp, up = _ffn(x_prompt.reshape(t, d), *ffn1, with_u=True)
    cos_p, sin_p = lane_tables(jnp.arange(s))
    pr = _project(up, cos_p, sin_p, pw, tm=512, batch=b)
    mk, mv = _memory_kv(mem_prompt.reshape(b * n_mem, d), row(g_mem[l]), w_mem_kv[l].astype(BF16), row(g_xk[l]))

    def bsd(a):
        return a.reshape(b, s, a.shape[-1])

    oa = _moba_prompt(bsd(pr["aq"]), bsd(pr["ak_b"]), pr["av_b"],
                      pr["kmean"].reshape(b, s // A_BLOCK, A_W).astype(BF16), slopes)
    om = _mla_prompt(bsd(pr["mq"]), bsd(pr["km"]), pr["vm"])
    ox = _cross_prompt(bsd(pr["xq"]), mk.reshape(b, n_mem, X_W), mv.reshape(b, n_mem, X_W))
    h2 = _merge(up, oa.reshape(t, A_W), om.reshape(t, M_W), ox.reshape(t, X_W), hp, mw)
    yp = _ffn(h2, *ffn2, with_u=False).reshape(b, s, d)

    ts = nseq * n_q
    hs, us = _ffn(x_sample.reshape(ts, d), *ffn1, with_u=True)
    pos_s = past_len + (jnp.arange(ts) % n_q)
    cos_s, sin_s = lane_tables(pos_s)
    ps = _project(us, cos_s, sin_s, pw, tm=min(ts, 512))

    nrow = n_q * A_HEADS
    hm5, rowc = _row_consts(n_q, A_HEADS, A_W, np.arange(A_W) // A_DH)
    hm7, _ = _row_consts(n_q, M_HEADS, M_QW, np.where(np.arange(M_QW) < M_W, np.arange(M_QW) // M_NOPE,
                                                        (np.arange(M_QW) % LANES) // M_HALF))

    def rep_rows(a, heads):
        return jnp.repeat(a.reshape(nseq, n_q, a.shape[-1]), heads, axis=1)

    def pad_new(a):
        a = a.reshape(nseq, n_q, a.shape[-1])
        return jnp.pad(a, ((0, 0), (0, LANES - n_q), (0, 0)))

    mq_s = ps["mq"].astype(F32)
    q_lo = mq_s[:, A_W:A_W + LANES].reshape(nseq, n_q, M_HEADS, M_HALF)
    q_hi = mq_s[:, A_W + LANES:].reshape(nseq, n_q, M_HEADS, M_HALF)
    g_r = g_mk[l][M_NOPE:].astype(F32)
    ar_cos = jnp.concatenate([q_lo, q_hi], axis=-1) * g_r
    ar_sin = jnp.concatenate([q_hi, -q_lo], axis=-1) * g_r
    ar = (jnp.concatenate([ar_cos, ar_sin], axis=-1) * M_SCALE).reshape(nseq, nrow, 2 * M_ROPE)

    c_all, s_all = _rope_tables(jnp.arange(past_len))
    n_blocks = past_len // A_BLOCK
    expand = (np.arange(past_len)[None, :] // A_BLOCK == np.arange(n_blocks)[:, None]).astype(np.float32)
    sc = {
        "wkn_t": wkn.T.astype(BF16),
        "wv": wv.astype(BF16),
        "g_nope": row(jnp.tile(g_mk[l][:M_NOPE], M_HEADS)),
        "hm5": jnp.asarray(hm5), "hm7": jnp.asarray(hm7), "rowc": jnp.asarray(rowc),
        "expand": jnp.asarray(expand, BF16),
        "cos_t": jnp.tile(c_all.T, (2, 1)), "sin_t": jnp.tile(s_all.T, (2, 1)),
    }
    sq = {
        "qa_rep": rep_rows(ps["aq"], A_HEADS),
        "qn_rep": rep_rows(ps["mq"][:, :M_W], M_HEADS),
        "qm_rep": rep_rows(ps["mq"], M_HEADS),
        "ar": ar,
        "ka_new": pad_new(ps["ak_b"]), "va_new": pad_new(ps["av_b"]),
        "km_new": pad_new(ps["km"]), "vm_new": pad_new(ps["vm"]),
    }
    n_pool = cache_moba_k.shape[1]

    def keys_on_lanes(c):
        c = c[l]
        return jnp.moveaxis(c, 1, -1).reshape(n_pool, -1, PAGE_SIZE)

    oa2, om2 = _sample_mixers(page_table, keys_on_lanes(cache_moba_k), keys_on_lanes(cache_moba_v),
                              cache_mla_ckv[l], keys_on_lanes(cache_mla_kr), sq, sc)
    ox2 = _cross_sample(ps["xq"].reshape(nseq, n_q * X_HEADS, X_DH),
                        cache_mem_k[l].reshape(nseq, n_mem * X_HEADS, X_DH),
                        cache_mem_v[l].reshape(nseq, n_mem * X_HEADS, X_DH))
    hs2 = _merge(us, oa2.reshape(ts, A_W), om2.reshape(ts, M_W), ox2.reshape(ts, X_W), hs, mw)
    ys = _ffn(hs2, *ffn2, with_u=False).reshape(nseq, n_q, d)

    def stack(a, shape):
        return a.reshape((1,) + shape)

    def tokens_back(a, shape):
        return jnp.moveaxis(a.reshape((b,) + shape + (s,)), -1, 1)[None]

    return (yp, ys,
            tokens_back(pr["ak"], (A_HEADS, A_DH)), tokens_back(pr["av"], (A_HEADS, A_DH)),
            stack(pr["ckv"], (b, s, M_KV_RANK)), tokens_back(pr["kr"], (M_ROPE,)),
            stack(mk, (b, n_mem, X_HEADS, X_DH)), stack(mv, (b, n_mem, X_HEADS, X_DH)),
            stack(ps["ak"], (nseq, n_q, A_HEADS, A_DH)), stack(ps["av"], (nseq, n_q, A_HEADS, A_DH)),
            stack(ps["ckv"], (nseq, n_q, M_KV_RANK)), stack(ps["kr"], (nseq, n_q, M_ROPE)))
```
